```python
import math
import jax
import jax.numpy as jnp
from jax import lax
import numpy as np

D_MODEL = 1024
BATCH = 8
SEQ = 2048
DEPTH = 2

SB_HEADS = 8
SB_HEAD_DIM = 64
SB_WIDTH = SB_HEADS * SB_HEAD_DIM
SB_BLOCK = 128
HG_HEADS = 4
HG_HEAD_DIM = 128
HG_WIDTH = HG_HEADS * HG_HEAD_DIM
HG_CHUNK = 64
CONV_WIDTH = 512
CONV_K = 3
N_BRANCH = 3
IN_COLS = 4 * SB_WIDTH + 4 * HG_WIDTH + 4 * CONV_WIDTH + N_BRANCH * D_MODEL
LN_EPS = 1e-5
RMS_EPS = 1e-6

kernel_name = "hybrid_sb_hgrn2_shortconv_deepnorm"


def _standardize(x):
    xf = x.astype(jnp.float32)
    mu = jnp.mean(xf, axis=-1, keepdims=True)
    xc = xf - mu
    var = jnp.mean(xc * xc, axis=-1, keepdims=True)
    return xc * lax.rsqrt(var + LN_EPS)


def _stick_breaking(q, k, v):
    B, S, H, dh = q.shape
    scale = dh ** -0.5
    qf = q.astype(jnp.float32)
    kf = k.astype(jnp.float32)
    vf = v.astype(jnp.float32)
    outs = []
    for blk in range(S // SB_BLOCK):
        t0 = blk * SB_BLOCK
        t1 = t0 + SB_BLOCK
        z = jnp.einsum('bthd,bshd->bhts', qf[:, t0:t1], kf[:, :t1]) * scale
        mask = jnp.arange(t1)[None, :] < (t0 + jnp.arange(SB_BLOCK))[:, None]
        log_1m_beta = jnp.where(mask, -jax.nn.softplus(z), 0.0)
        log_surv = lax.cumsum(log_1m_beta, axis=3, reverse=True) - log_1m_beta
        a = jnp.where(mask, jnp.exp(jax.nn.log_sigmoid(z) + log_surv), 0.0)
        outs.append(jnp.einsum('bhts,bshd->bthd', a, vf[:, :t1]))
    return jnp.concatenate(outs, axis=1)


def _hgrn2(q, f_pre, i_in, lb):
    B, S, H, dk = q.shape
    dv = i_in.shape[-1]
    n_chunks = S // HG_CHUNK
    f = lb + (1.0 - lb) * jax.nn.sigmoid(f_pre.astype(jnp.float32))
    k = 1.0 - f
    g = jnp.log(f)

    def to_chunks(a):
        return a.reshape(B, n_chunks, HG_CHUNK, H, a.shape[-1]).transpose(1, 0, 3, 2, 4)

    qc = to_chunks(q.astype(jnp.float32))
    kc = to_chunks(k)
    vc = to_chunks(i_in.astype(jnp.float32))
    bc = jnp.cumsum(to_chunks(g), axis=3)
    causal = jnp.tril(jnp.ones((HG_CHUNK, HG_CHUNK), dtype=bool))

    def step(state, inp):
        q_c, k_c, v_c, b_c = inp
        inter = jnp.einsum('bhtd,bhde->bhte', q_c * jnp.exp(b_c), state)
        diff = b_c[:, :, :, None, :] - b_c[:, :, None, :, :]
        decay = jnp.exp(jnp.where(causal[:, :, None], diff, -jnp.inf))
        scores = jnp.einsum('bhtsd,bhsd->bhts', q_c[:, :, :, None, :] * decay, k_c)
        intra = jnp.einsum('bhts,bhse->bhte', scores, v_c)
        b_end = b_c[:, :, -1, :]
        k_dec = k_c * jnp.exp(b_end[:, :, None, :] - b_c)
        new_state = jnp.exp(b_end)[..., None] * state + jnp.einsum('bhsd,bhse->bhde', k_dec, v_c)
        return new_state, inter + intra

    state0 = jnp.zeros((B, H, dk, dv), jnp.float32)
    _, o = lax.scan(step, state0, (qc, kc, vc, bc))
    return o.transpose(1, 0, 3, 2, 4).reshape(B, S, H, dv)


def _short_conv(u, w):
    ch = u.shape[-1]
    return lax.conv_general_dilated(
        u, w[:, None, :].astype(u.dtype), window_strides=(1,), padding=[(CONV_K - 1, 0)],
        dimension_numbers=('NWC', 'WIO', 'NWC'), feature_group_count=ch)


def setup_inputs(seed: int = 0) -> dict:
    key = jax.random.key(seed)
    ks = jax.random.split(key, 12)
    beta = (8.0 * DEPTH) ** -0.25

    def nrm(k, shape, scale):
        return jax.random.normal(k, shape, jnp.float32) * scale

    return {
        "x": nrm(ks[0], (BATCH, SEQ, D_MODEL), 1.0),
        "c": nrm(ks[1], (BATCH, D_MODEL), 1.0),
        "w_mod": nrm(ks[2], (DEPTH, D_MODEL, 3 * D_MODEL), 0.2 * D_MODEL ** -0.5),
        "b_mod": nrm(ks[3], (DEPTH, 3 * D_MODEL), 0.01),
        "w_in": nrm(ks[4], (DEPTH, D_MODEL, IN_COLS), D_MODEL ** -0.5),
        "conv_w": nrm(ks[5], (DEPTH, CONV_K, CONV_WIDTH), CONV_K ** -0.5),
        "hgrn_norm_w": 1.0 + nrm(ks[6], (DEPTH, HG_HEAD_DIM), 0.02),
        "lower_bounds": nrm(ks[7], (DEPTH, HG_WIDTH), 0.1),
        "w_branch": nrm(ks[8], (DEPTH, N_BRANCH, SB_WIDTH, D_MODEL), beta * SB_WIDTH ** -0.5),
        "w_out": nrm(ks[9], (DEPTH, D_MODEL, D_MODEL), beta * D_MODEL ** -0.5),
        "ln_g": 1.0 + nrm(ks[10], (DEPTH, D_MODEL), 0.02),
        "ln_b": nrm(ks[11], (DEPTH, D_MODEL), 0.02),
    }


def reference(x, c, w_mod, b_mod, w_in, conv_w, hgrn_norm_w, lower_bounds, w_branch, w_out, ln_g, ln_b):
    B, S, D = x.shape
    dt = x.dtype
    alpha = (2.0 * DEPTH) ** 0.25
    p = jax.nn.softmax(lower_bounds.astype(jnp.float32), axis=0)
    lbs = jnp.cumsum(p, axis=0) - p[0:1]
    sizes = [SB_WIDTH] * 4 + [HG_WIDTH] * 4 + [CONV_WIDTH] * 4 + [D_MODEL] * N_BRANCH
    splits = np.cumsum(sizes[:-1]).tolist()

    for l in range(DEPTH):
        mod = (c @ w_mod[l] + b_mod[l])[:, None, :]
        shift, scale, gate = jnp.split(mod.astype(jnp.float32), 3, axis=-1)
        h = (_standardize(x) * (1.0 + scale) + shift).astype(dt)

        proj = h @ w_in[l]
        (q_a, k_a, v_a, z_a, q_b, f_b, i_b, z_b,
         pre_c, post_c, u_c, z_c, g_a, g_b, g_c) = jnp.split(proj, splits, axis=-1)

        o_a = _stick_breaking(q_a.reshape(B, S, SB_HEADS, SB_HEAD_DIM),
                              k_a.reshape(B, S, SB_HEADS, SB_HEAD_DIM),
                              v_a.reshape(B, S, SB_HEADS, SB_HEAD_DIM)).reshape(B, S, SB_WIDTH)
        y_a = (o_a * jax.nn.silu(z_a.astype(jnp.float32))).astype(dt)

        o_b = _hgrn2(jax.nn.silu(q_b).reshape(B, S, HG_HEADS, HG_HEAD_DIM),
                     f_b.reshape(B, S, HG_HEADS, HG_HEAD_DIM),
                     i_b.reshape(B, S, HG_HEADS, HG_HEAD_DIM),
                     lbs[l].reshape(HG_HEADS, HG_HEAD_DIM))
        o_b = o_b * lax.rsqrt(jnp.mean(o_b * o_b, axis=-1, keepdims=True) + RMS_EPS)
        o_b = (o_b * hgrn_norm_w[l].astype(jnp.float32)).reshape(B, S, HG_WIDTH)
        y_b = (o_b * jax.nn.silu(z_b.astype(jnp.float32))).astype(dt)

        y_c = post_c * _short_conv(pre_c * u_c, conv_w[l]) * jax.nn.silu(z_c)

        merged = (jax.nn.sigmoid(g_a) * (y_a @ w_branch[l, 0])
                  + jax.nn.sigmoid(g_b) * (y_b @ w_branch[l, 1])
                  + jax.nn.sigmoid(g_c) * (y_c.astype(dt) @ w_branch[l, 2]))
        y = (merged @ w_out[l]).astype(jnp.float32)

        r = alpha * x.astype(jnp.float32) + (1.0 + gate) * y
        x = (_standardize(r) * ln_g[l] + ln_b[l]).astype(dt)
    return x
```

```python
from functools import partial

import numpy as np
import jax
import jax.numpy as jnp
from jax import lax
from jax.experimental import pallas as pl
from jax.experimental.pallas import tpu as pltpu

D_MODEL = 1024
SB_HEADS = 8
SB_HEAD_DIM = 64
SB_WIDTH = SB_HEADS * SB_HEAD_DIM
HG_HEADS = 4
HG_HEAD_DIM = 128
HG_WIDTH = HG_HEADS * HG_HEAD_DIM
CONV_WIDTH = 512
CONV_K = 3
LN_EPS = 1e-5
RMS_EPS = 1e-6

F32 = jnp.float32
BF16 = jnp.bfloat16

V7X_VMEM_LIMIT_BYTES = 56 * 1024 * 1024
SUBLANES = 8

PA_COLS = 4 * 512
PF_COLS = 8 * 512 + 3 * D_MODEL

SB_GROUP = 4
SB_GROUP_W = SB_GROUP * SB_HEAD_DIM
SB_TQ = 128
SB_TK = 256

HG_C = 128
HG_LEVELS = 7
HG_NMAT = HG_LEVELS + 2


def _nt_dot(a, b):
    return lax.dot_general(a, b, (((1,), (1,)), ((), ())), preferred_element_type=F32)


def _dot(a, b):
    return jnp.dot(a, b, preferred_element_type=F32)


def _split_bf16(v):
    hi = v.astype(BF16)
    lo = (v - hi.astype(F32)).astype(BF16)
    return hi, lo


def _sigmoid(v):
    return 1.0 / (1.0 + jnp.exp(-v))


def _silu(v):
    return v * _sigmoid(v)


def _standardize(xf):
    mu = jnp.mean(xf, axis=-1, keepdims=True)
    xc = xf - mu
    var = jnp.mean(xc * xc, axis=-1, keepdims=True)
    return xc * lax.rsqrt(var + LN_EPS)


def _mod_kernel(c_ref, w_ref, b_ref, o_ref):
    c_hi, c_lo = _split_bf16(c_ref[...])
    w_hi, w_lo = _split_bf16(w_ref[0])
    acc = _dot(c_hi, w_hi) + (_dot(c_hi, w_lo) + _dot(c_lo, w_hi))
    o_ref[0] = acc + b_ref[0]


def _modulation(c, w_mod, b_mod):
    depth, d, n = w_mod.shape
    bsz = c.shape[0]
    tn = 1024
    return pl.pallas_call(
        _mod_kernel,
        out_shape=jax.ShapeDtypeStruct((depth, bsz, n), F32),
        grid=(depth, n // tn),
        in_specs=[
            pl.BlockSpec((bsz, d), lambda l, j: (0, 0)),
            pl.BlockSpec((1, d, tn), lambda l, j: (l, 0, j)),
            pl.BlockSpec((1, 1, tn), lambda l, j: (l, 0, j)),
        ],
        out_specs=pl.BlockSpec((1, bsz, tn), lambda l, j: (l, 0, j)),
        compiler_params=pltpu.CompilerParams(
            dimension_semantics=("arbitrary", "arbitrary"),
            vmem_limit_bytes=V7X_VMEM_LIMIT_BYTES),
        name="adaln_mod",
    )(c, w_mod, b_mod.reshape(depth, 1, n))


def _lnmod_kernel(x_ref, mod_ref, h_ref):
    d = x_ref.shape[-1]
    shift = mod_ref[0, :, 0:d]
    scale = mod_ref[0, :, d:2 * d]
    h_ref[0] = (_standardize(x_ref[0]) * (1.0 + scale) + shift).astype(h_ref.dtype)


def _lnmod(x, mod_l):
    bsz, s, d = x.shape
    tm = 512
    return pl.pallas_call(
        _lnmod_kernel,
        out_shape=jax.ShapeDtypeStruct((bsz, s, d), BF16),
        grid=(bsz, s // tm),
        in_specs=[
            pl.BlockSpec((1, tm, d), lambda b, i: (b, i, 0)),
            pl.BlockSpec((1, 1, 3 * d), lambda b, i: (b, 0, 0)),
        ],
        out_specs=pl.BlockSpec((1, tm, d), lambda b, i: (b, i, 0)),
        compiler_params=pltpu.CompilerParams(
            dimension_semantics=("arbitrary", "arbitrary"),
            vmem_limit_bytes=V7X_VMEM_LIMIT_BYTES),
        name="ln_modulate",
    )(x, mod_l)


def _matmul_kernel(a_ref, w_ref, o_ref):
    o_ref[...] = _dot(a_ref[...], w_ref[...]).astype(o_ref.dtype)


def _matmul(a, w, out_dtype, tm, tn, name):
    m, k = a.shape
    n = w.shape[1]
    return pl.pallas_call(
        _matmul_kernel,
        out_shape=jax.ShapeDtypeStruct((m, n), out_dtype),
        grid=(m // tm, n // tn),
        in_specs=[
            pl.BlockSpec((tm, k), lambda i, j: (i, 0)),
            pl.BlockSpec((k, tn), lambda i, j: (0, j)),
        ],
        out_specs=pl.BlockSpec((tm, tn), lambda i, j: (i, j)),
        compiler_params=pltpu.CompilerParams(
            dimension_semantics=("arbitrary", "arbitrary"),
            vmem_limit_bytes=V7X_VMEM_LIMIT_BYTES),
        name=name,
    )(a, w)


def _sb_kernel(q_ref, k_ref, v_ref, z_ref, o_ref, acc_ref, car_ref):
    qi = pl.program_id(2)
    tq, tk, gw = SB_TQ, SB_TK, SB_GROUP_W
    rows = SB_GROUP * tq

    qf = q_ref[0].astype(F32) * (SB_HEAD_DIM ** -0.5)
    lane_head = lax.broadcasted_iota(jnp.int32, (tq, gw), 1) // SB_HEAD_DIM
    qs = jnp.concatenate(
        [jnp.where(lane_head == h, qf, 0.0) for h in range(SB_GROUP)], axis=0).astype(BF16)

    jj = lax.broadcasted_iota(jnp.int32, (tk, tk), 0)
    ss = lax.broadcasted_iota(jnp.int32, (tk, tk), 1)
    u_mat = jnp.where(jj >= ss, 1.0, 0.0).astype(BF16)

    acc_ref[...] = jnp.zeros_like(acc_ref)
    car_ref[...] = jnp.zeros_like(car_ref)

    def block(kb, masked):
        start = pl.multiple_of(kb * tk, tk)
        kblk = k_ref[0, pl.ds(start, tk), :]
        vblk = v_ref[0, pl.ds(start, tk), :]
        z = _nt_dot(qs, kblk)
        sp = jnp.maximum(z, 0.0) + jnp.log(1.0 + jnp.exp(-jnp.abs(z)))
        if masked:
            t_idx = qi * tq + lax.broadcasted_iota(jnp.int32, (rows, tk), 0) % tq
            s_idx = kb * tk + lax.broadcasted_iota(jnp.int32, (rows, tk), 1)
            mask = s_idx < t_idx
            sp = jnp.where(mask, sp, 0.0)
        sp_hi, sp_lo = _split_bf16(sp)
        rb = _dot(sp_hi, u_mat) + _dot(sp_lo, u_mat)
        a = jnp.exp(z - (rb + car_ref[...]))
        if masked:
            a = jnp.where(mask, a, 0.0)
        acc_ref[...] += _dot(a.astype(BF16), vblk)
        car_ref[...] += rb[:, 0:1]

    n_kb = (qi * tq) // tk + 1
    block(n_kb - 1, True)

    def body(i, carry):
        block(n_kb - 2 - i, False)
        return carry

    lax.fori_loop(0, n_kb - 1, body, 0)

    acc = acc_ref[...]
    o = jnp.zeros((tq, gw), F32)
    for h in range(SB_GROUP):
        o = o + jnp.where(lane_head == h, acc[h * tq:(h + 1) * tq], 0.0)
    o_ref[0] = (o * _silu(z_ref[0])).astype(o_ref.dtype)


def _stick_breaking(pa, pf):
    bsz, s, _ = pa.shape
    n_grp = SB_WIDTH // SB_GROUP_W
    gw = SB_GROUP_W
    return pl.pallas_call(
        _sb_kernel,
        out_shape=jax.ShapeDtypeStruct((bsz, s, SB_WIDTH), BF16),
        grid=(bsz, n_grp, s // SB_TQ),
        in_specs=[
            pl.BlockSpec((1, SB_TQ, gw), lambda b, g, i: (b, i, g)),
            pl.BlockSpec((1, s, gw), lambda b, g, i: (b, 0, n_grp + g)),
            pl.BlockSpec((1, s, gw), lambda b, g, i: (b, 0, 2 * n_grp + g)),
            pl.BlockSpec((1, SB_TQ, gw), lambda b, g, i: (b, i, g)),
        ],
        out_specs=pl.BlockSpec((1, SB_TQ, gw), lambda b, g, i: (b, i, g)),
        scratch_shapes=[
            pltpu.VMEM((SB_GROUP * SB_TQ, gw), F32),
            pltpu.VMEM((SB_GROUP * SB_TQ, 1), F32),
        ],
        compiler_params=pltpu.CompilerParams(
            dimension_semantics=("arbitrary", "arbitrary", "arbitrary"),
            vmem_limit_bytes=V7X_VMEM_LIMIT_BYTES),
        name="stick_breaking",
    )(pa, pa, pa, pf)


def _hgrn_sum_matrix():
    c = HG_C
    w = np.zeros((HG_NMAT, c, c), np.float32)
    for lvl in range(1, HG_LEVELS + 1):
        n = 1 << lvl
        for r in range(c):
            m = (r // n) * n + n // 2
            if r >= m:
                w[lvl - 1, r, m:r + 1] = 1.0
            else:
                w[lvl - 1, r, r + 1:m] = 1.0
    for r in range(c):
        w[HG_LEVELS, r, :r + 1] = 1.0
        w[HG_LEVELS + 1, r, r + 1:] = 1.0
    return w.reshape(HG_NMAT * c, c)


def _hgrn_kernel(layer, q_ref, f_ref, i_ref, z_ref, lb_ref, nw_ref, w_ref, o_ref, state_ref):
    c = HG_C
    ci = pl.program_id(2)

    @pl.when(ci == 0)
    def _():
        state_ref[...] = jnp.zeros_like(state_ref)

    lbw = lb_ref[0]
    e = jnp.exp(lbw - jnp.max(lbw, axis=0, keepdims=True))
    p = e / jnp.sum(e, axis=0, keepdims=True)
    lb = jnp.zeros((1, HG_HEAD_DIM), F32)
    for l in range(1, layer + 1):
        lb = lb + p[l:l + 1]

    f = lb + (1.0 - lb) * _sigmoid(f_ref[0])
    kk = 1.0 - f
    g = jnp.log(f)
    q = _silu(q_ref[0])
    v_bf = i_ref[0]

    g_hi, g_lo = _split_bf16(g)
    e2 = _dot(w_ref[...], jnp.concatenate([g_hi, g_lo], axis=1))
    ex = jnp.exp(e2[:, :HG_HEAD_DIM] + e2[:, HG_HEAD_DIM:])

    ti = lax.broadcasted_iota(jnp.int32, (c, c), 0)
    si = lax.broadcasted_iota(jnp.int32, (c, c), 1)
    scores = jnp.where(ti == si, _nt_dot(q.astype(BF16), kk.astype(BF16)), 0.0)
    for lvl in range(1, HG_LEVELS + 1):
        dec = ex[(lvl - 1) * c:lvl * c]
        p_l = _nt_dot((q * dec).astype(BF16), (kk * dec).astype(BF16))
        in_block = (ti >> lvl) == (si >> lvl)
        t_up = ((ti >> (lvl - 1)) & 1) == 1
        s_lo = ((si >> (lvl - 1)) & 1) == 0
        scores = scores + jnp.where(in_block & t_up & s_lo, p_l, 0.0)

    dec_q = ex[HG_LEVELS * c:(HG_LEVELS + 1) * c]
    dec_k = ex[(HG_LEVELS + 1) * c:(HG_LEVELS + 2) * c]
    state = state_ref[...]
    inter = _nt_dot((q * dec_q).astype(BF16), state.astype(BF16))
    intra = _dot(scores.astype(BF16), v_bf)
    o = inter + intra

    k_dec = (kk * dec_k).astype(BF16)
    v_t = v_bf.astype(F32).T.astype(BF16)
    state_ref[...] = state * dec_q[c - 1:c, :] + _dot(v_t, k_dec)

    ms = jnp.mean(o * o, axis=-1, keepdims=True)
    o = o * lax.rsqrt(ms + RMS_EPS) * nw_ref[...]
    o_ref[0] = (o * _silu(z_ref[0])).astype(o_ref.dtype)


def _hgrn2(pa, pf, lower_bounds, norm_w_l, layer):
    bsz, s, _ = pa.shape
    depth = lower_bounds.shape[0]
    dk = HG_HEAD_DIM
    lb_t = lower_bounds.reshape(depth, HG_HEADS, dk).transpose(1, 0, 2)
    w_sum = jnp.asarray(_hgrn_sum_matrix(), BF16)
    pa_col = 3 * 512 // dk
    pf_q, pf_f, pf_z = 512 // dk, 1024 // dk, 1536 // dk
    return pl.pallas_call(
        partial(_hgrn_kernel, layer),
        out_shape=jax.ShapeDtypeStruct((bsz, s, HG_WIDTH), BF16),
        grid=(bsz, HG_HEADS, s // HG_C),
        in_specs=[
            pl.BlockSpec((1, HG_C, dk), lambda b, h, i: (b, i, pf_q + h)),
            pl.BlockSpec((1, HG_C, dk), lambda b, h, i: (b, i, pf_f + h)),
            pl.BlockSpec((1, HG_C, dk), lambda b, h, i: (b, i, pa_col + h)),
            pl.BlockSpec((1, HG_C, dk), lambda b, h, i: (b, i, pf_z + h)),
            pl.BlockSpec((1, depth, dk), lambda b, h, i: (h, 0, 0)),
            pl.BlockSpec((1, dk), lambda b, h, i: (0, 0)),
            pl.BlockSpec((HG_NMAT * HG_C, HG_C), lambda b, h, i: (0, 0)),
        ],
        out_specs=pl.BlockSpec((1, HG_C, dk), lambda b, h, i: (b, i, h)),
        scratch_shapes=[pltpu.VMEM((dk, dk), F32)],
        compiler_params=pltpu.CompilerParams(
            dimension_semantics=("arbitrary", "arbitrary", "arbitrary"),
            vmem_limit_bytes=V7X_VMEM_LIMIT_BYTES),
        name="hgrn2",
    )(pf, pf, pa, pf, lb_t, norm_w_l.reshape(1, dk), w_sum)


def _merge_kernel(alpha, emit_h, ya_ref, yb_ref, pre_ref, post_ref, u_ref, zc_ref,
                  pre_h_ref, u_h_ref, ga_ref, gb_ref, gc_ref, x_ref, gate_ref, cw_ref,
                  wb_ref, wo_ref, lng_ref, lnb_ref, *rest):
    if emit_h:
        nmod_ref, xo_ref, h_ref, ext_ref = rest
    else:
        xo_ref, ext_ref = rest
    i = pl.program_id(1)
    tm = pre_ref.shape[1]

    halo = jnp.where(i > 0, pre_h_ref[0] * u_h_ref[0], 0.0)
    ext_ref[0:SUBLANES, :] = halo
    ext_ref[SUBLANES:SUBLANES + tm, :] = pre_ref[0] * u_ref[0]
    cw = cw_ref[...]
    conv = (cw[2:3] * ext_ref[SUBLANES:SUBLANES + tm, :]
            + cw[1:2] * ext_ref[SUBLANES - 1:SUBLANES - 1 + tm, :]
            + cw[0:1] * ext_ref[SUBLANES - 2:SUBLANES - 2 + tm, :])
    y_c = (post_ref[0] * conv * _silu(zc_ref[0])).astype(BF16)

    merged = (_sigmoid(ga_ref[0]) * _dot(ya_ref[0], wb_ref[0])
              + _sigmoid(gb_ref[0]) * _dot(yb_ref[0], wb_ref[1])
              + _sigmoid(gc_ref[0]) * _dot(y_c, wb_ref[2]))
    y = _dot(merged.astype(BF16), wo_ref[...])

    r = alpha * x_ref[0] + (1.0 + gate_ref[0]) * y
    x_new = _standardize(r) * lng_ref[...] + lnb_ref[...]
    xo_ref[0] = x_new
    if emit_h:
        d = x_new.shape[-1]
        shift = nmod_ref[0, :, 0:d]
        scale = nmod_ref[0, :, d:2 * d]
        h_ref[0] = (_standardize(x_new) * (1.0 + scale) + shift).astype(h_ref.dtype)


def _merge(ya, yb, pf, x, mod_l, conv_w_l, wb_l, wo_l, lng_l, lnb_l, alpha, next_mod):
    bsz, s, d = x.shape
    tm = 512
    cw = CONV_WIDTH
    emit_h = next_mod is not None
    halo_blocks = tm // SUBLANES

    def halo_map(col):
        return lambda b, i: (b, jnp.maximum(i * halo_blocks - 1, 0), col)

    in_specs = [
        pl.BlockSpec((1, tm, SB_WIDTH), lambda b, i: (b, i, 0)),
        pl.BlockSpec((1, tm, HG_WIDTH), lambda b, i: (b, i, 0)),
        pl.BlockSpec((1, tm, cw), lambda b, i: (b, i, 4)),
        pl.BlockSpec((1, tm, cw), lambda b, i: (b, i, 5)),
        pl.BlockSpec((1, tm, cw), lambda b, i: (b, i, 6)),
        pl.BlockSpec((1, tm, cw), lambda b, i: (b, i, 7)),
        pl.BlockSpec((1, SUBLANES, cw), halo_map(4)),
        pl.BlockSpec((1, SUBLANES, cw), halo_map(6)),
        pl.BlockSpec((1, tm, d), lambda b, i: (b, i, 4)),
        pl.BlockSpec((1, tm, d), lambda b, i: (b, i, 5)),
        pl.BlockSpec((1, tm, d), lambda b, i: (b, i, 6)),
        pl.BlockSpec((1, tm, d), lambda b, i: (b, i, 0)),
        pl.BlockSpec((1, 1, d), lambda b, i: (b, 0, 2)),
        pl.BlockSpec((CONV_K, cw), lambda b, i: (0, 0)),
        pl.BlockSpec((3, SB_WIDTH, d), lambda b, i: (0, 0, 0)),
        pl.BlockSpec((d, d), lambda b, i: (0, 0)),
        pl.BlockSpec((1, d), lambda b, i: (0, 0)),
        pl.BlockSpec((1, d), lambda b, i: (0, 0)),
    ]
    args = [ya, yb, pf, pf, pf, pf, pf, pf, pf, pf, pf, x, mod_l, conv_w_l, wb_l, wo_l,
            lng_l.reshape(1, d), lnb_l.reshape(1, d)]
    out_shape = [jax.ShapeDtypeStruct((bsz, s, d), F32)]
    out_specs = [pl.BlockSpec((1, tm, d), lambda b, i: (b, i, 0))]
    if emit_h:
        in_specs.append(pl.BlockSpec((1, 1, 3 * d), lambda b, i: (b, 0, 0)))
        args.append(next_mod)
        out_shape.append(jax.ShapeDtypeStruct((bsz, s, d), BF16))
        out_specs.append(pl.BlockSpec((1, tm, d), lambda b, i: (b, i, 0)))
    outs = pl.pallas_call(
        partial(_merge_kernel, alpha, emit_h),
        out_shape=out_shape,
        grid=(bsz, s // tm),
        in_specs=in_specs,
        out_specs=out_specs,
        scratch_shapes=[pltpu.VMEM((tm + SUBLANES, cw), F32)],
        compiler_params=pltpu.CompilerParams(
            dimension_semantics=("arbitrary", "arbitrary"),
            vmem_limit_bytes=V7X_VMEM_LIMIT_BYTES),
        name="merge_residual",
    )(*args)
    return (outs[0], outs[1]) if emit_h else (outs[0], None)


def _split_w_in(w_in_l):
    blk = [w_in_l[:, i * 512:(i + 1) * 512] for i in range(12)]
    gates = w_in_l[:, 12 * 512:]
    w_a = jnp.concatenate([blk[0], blk[1], blk[2], blk[6]], axis=1)
    w_f = jnp.concatenate([blk[3], blk[4], blk[5], blk[7], blk[8], blk[9], blk[10], blk[11],
                           gates], axis=1)
    return w_a.astype(BF16), w_f.astype(BF16)


def kernel(x, c, w_mod, b_mod, w_in, conv_w, hgrn_norm_w, lower_bounds, w_branch, w_out, ln_g, ln_b):
    bsz, s, d = x.shape
    depth = w_mod.shape[0]
    alpha = (2.0 * depth) ** 0.25

    mod = _modulation(c, w_mod, b_mod)
    mods = [mod[l].reshape(bsz, 1, 3 * d) for l in range(depth)]

    h = _lnmod(x, mods[0])
    for l in range(depth):
        w_a, w_f = _split_w_in(w_in[l])
        h2 = h.reshape(bsz * s, d)
        pa = _matmul(h2, w_a, BF16, 1024, 1024, "in_proj_bf16").reshape(bsz, s, PA_COLS)
        pf = _matmul(h2, w_f, F32, 1024, 1024, "in_proj_f32").reshape(bsz, s, PF_COLS)
        ya = _stick_breaking(pa, pf)
        yb = _hgrn2(pa, pf, lower_bounds, hgrn_norm_w[l], l)
        next_mod = mods[l + 1] if l + 1 < depth else None
        x, h = _merge(ya, yb, pf, x, mods[l], conv_w[l], w_branch[l].astype(BF16),
                      w_out[l].astype(BF16), ln_g[l], ln_b[l], alpha, next_mod)
    return x
```

```python
from functools import partial

import numpy as np
import jax
import jax.numpy as jnp
from jax import lax
from jax.experimental import pallas as pl
from jax.experimental.pallas import tpu as pltpu

D_MODEL = 1024
SB_HEADS = 8
SB_HEAD_DIM = 64
SB_WIDTH = SB_HEADS * SB_HEAD_DIM
HG_HEADS = 4
HG_HEAD_DIM = 128
HG_WIDTH = HG_HEADS * HG_HEAD_DIM
CONV_WIDTH = 512
CONV_K = 3
LN_EPS = 1e-5
RMS_EPS = 1e-6

F32 = jnp.float32
BF16 = jnp.bfloat16

V7X_VMEM_LIMIT_BYTES = 56 * 1024 * 1024
SUBLANES = 8

PA_COLS = 4 * 512
PF_COLS = 8 * 512 + 3 * D_MODEL

SB_GROUP = 4
SB_GROUP_W = SB_GROUP * SB_HEAD_DIM
SB_N_GROUPS = SB_HEADS // SB_GROUP
SB_SUB = 128
SB_TQ = 256
SB_TK = 256
SB_CHAINS = [(sub, g) for g in range(SB_N_GROUPS) for sub in range(SB_TQ // SB_SUB)]
LOG2E = 1.4426950408889634

HG_C = 128
HG_LEVELS = 7
HG_NMAT = HG_LEVELS + 2


def _nt_dot(a, b):
    return lax.dot_general(a, b, (((1,), (1,)), ((), ())), preferred_element_type=F32)


def _dot(a, b):
    return jnp.dot(a, b, preferred_element_type=F32)


def _split_bf16(v):
    hi = v.astype(BF16)
    lo = (v - hi.astype(F32)).astype(BF16)
    return hi, lo


def _sigmoid(v):
    return 1.0 / (1.0 + jnp.exp(-v))


def _silu(v):
    return v * _sigmoid(v)


def _standardize(xf):
    mu = jnp.mean(xf, axis=-1, keepdims=True)
    xc = xf - mu
    var = jnp.mean(xc * xc, axis=-1, keepdims=True)
    return xc * lax.rsqrt(var + LN_EPS)


def _mod_kernel(c_ref, w_ref, b_ref, o_ref):
    c_hi, c_lo = _split_bf16(c_ref[...])
    w_hi, w_lo = _split_bf16(w_ref[0])
    acc = _dot(c_hi, w_hi) + (_dot(c_hi, w_lo) + _dot(c_lo, w_hi))
    o_ref[0] = acc + b_ref[0]


def _modulation(c, w_mod, b_mod):
    depth, d, n = w_mod.shape
    bsz = c.shape[0]
    tn = 1024
    return pl.pallas_call(
        _mod_kernel,
        out_shape=jax.ShapeDtypeStruct((depth, bsz, n), F32),
        grid=(depth, n // tn),
        in_specs=[
            pl.BlockSpec((bsz, d), lambda l, j: (0, 0)),
            pl.BlockSpec((1, d, tn), lambda l, j: (l, 0, j)),
            pl.BlockSpec((1, 1, tn), lambda l, j: (l, 0, j)),
        ],
        out_specs=pl.BlockSpec((1, bsz, tn), lambda l, j: (l, 0, j)),
        compiler_params=pltpu.CompilerParams(
            dimension_semantics=("arbitrary", "arbitrary"),
            vmem_limit_bytes=V7X_VMEM_LIMIT_BYTES),
        name="adaln_mod",
    )(c, w_mod, b_mod.reshape(depth, 1, n))


def _lnmod_kernel(x_ref, mod_ref, h_ref):
    d = x_ref.shape[-1]
    shift = mod_ref[0, :, 0:d]
    scale = mod_ref[0, :, d:2 * d]
    h_ref[0] = (_standardize(x_ref[0]) * (1.0 + scale) + shift).astype(h_ref.dtype)


def _lnmod(x, mod_l):
    bsz, s, d = x.shape
    tm = 512
    return pl.pallas_call(
        _lnmod_kernel,
        out_shape=jax.ShapeDtypeStruct((bsz, s, d), BF16),
        grid=(bsz, s // tm),
        in_specs=[
            pl.BlockSpec((1, tm, d), lambda b, i: (b, i, 0)),
            pl.BlockSpec((1, 1, 3 * d), lambda b, i: (b, 0, 0)),
        ],
        out_specs=pl.BlockSpec((1, tm, d), lambda b, i: (b, i, 0)),
        compiler_params=pltpu.CompilerParams(
            dimension_semantics=("arbitrary", "arbitrary"),
            vmem_limit_bytes=V7X_VMEM_LIMIT_BYTES),
        name="ln_modulate",
    )(x, mod_l)


def _matmul_kernel(a_ref, w_ref, o_ref):
    o_ref[...] = _dot(a_ref[...], w_ref[...]).astype(o_ref.dtype)


def _matmul(a, w, out_dtype, tm, tn, name):
    m, k = a.shape
    n = w.shape[1]
    return pl.pallas_call(
        _matmul_kernel,
        out_shape=jax.ShapeDtypeStruct((m, n), out_dtype),
        grid=(m // tm, n // tn),
        in_specs=[
            pl.BlockSpec((tm, k), lambda i, j: (i, 0)),
            pl.BlockSpec((k, tn), lambda i, j: (0, j)),
        ],
        out_specs=pl.BlockSpec((tm, tn), lambda i, j: (i, j)),
        compiler_params=pltpu.CompilerParams(
            dimension_semantics=("arbitrary", "arbitrary"),
            vmem_limit_bytes=V7X_VMEM_LIMIT_BYTES),
        name=name,
    )(a, w)


def _sb_kernel(q_ref, k_ref, v_ref, z_ref, o_ref, qs_ref, acc_ref, car_ref):
    qi = pl.program_id(1)
    sub_q, tk, gw = SB_SUB, SB_TK, SB_GROUP_W
    rows = SB_GROUP * sub_q

    lane_head = lax.broadcasted_iota(jnp.int32, (sub_q, gw), 1) // SB_HEAD_DIM
    for c, (sub, g) in enumerate(SB_CHAINS):
        qf = q_ref[0, sub * sub_q:(sub + 1) * sub_q, g * gw:(g + 1) * gw].astype(F32)
        qs_ref[c] = jnp.concatenate(
            [jnp.where(lane_head == h, qf, 0.0) for h in range(SB_GROUP)], axis=0).astype(BF16)

    jj = lax.broadcasted_iota(jnp.int32, (tk, tk), 0)
    ss = lax.broadcasted_iota(jnp.int32, (tk, tk), 1)
    u_mat = jnp.where(jj >= ss, 1.0, 0.0).astype(BF16)

    acc_ref[...] = jnp.zeros_like(acc_ref)
    car_ref[...] = jnp.zeros_like(car_ref)

    def block(kb, masked):
        start = pl.multiple_of(kb * tk, tk)
        for c, (sub, g) in enumerate(SB_CHAINS):
            kblk = k_ref[0, pl.ds(start, tk), g * gw:(g + 1) * gw]
            vblk = v_ref[0, pl.ds(start, tk), g * gw:(g + 1) * gw]
            z = _nt_dot(qs_ref[c], kblk)
            sp = jnp.maximum(z, 0.0) + jnp.log2(1.0 + jnp.exp2(-jnp.abs(z)))
            if masked:
                t_in = sub * sub_q + lax.broadcasted_iota(jnp.int32, (rows, tk), 0) % sub_q
                mask = lax.broadcasted_iota(jnp.int32, (rows, tk), 1) < t_in
                sp = jnp.where(mask, sp, 0.0)
            rb = _dot(sp.astype(BF16), u_mat)
            a = jnp.exp2(z - (rb + car_ref[c]))
            if masked:
                a = jnp.where(mask, a, 0.0)
            acc_ref[c] += _dot(a.astype(BF16), vblk)
            car_ref[c] += rb[:, 0:1]

    block(qi, True)

    def body(i, carry):
        block(qi - 1 - i, False)
        return carry

    lax.fori_loop(0, qi, body, 0)

    for c, (sub, g) in enumerate(SB_CHAINS):
        acc = acc_ref[c]
        o = jnp.zeros((sub_q, gw), F32)
        for h in range(SB_GROUP):
            o = o + jnp.where(lane_head == h, acc[h * sub_q:(h + 1) * sub_q], 0.0)
        zg = z_ref[0, sub * sub_q:(sub + 1) * sub_q, g * gw:(g + 1) * gw]
        o_ref[0, sub * sub_q:(sub + 1) * sub_q, g * gw:(g + 1) * gw] = (
            o * _silu(zg)).astype(o_ref.dtype)


def _stick_breaking(pa, pf):
    bsz, s, _ = pa.shape
    w = SB_WIDTH
    n_chains = len(SB_CHAINS)
    rows = SB_GROUP * SB_SUB
    return pl.pallas_call(
        _sb_kernel,
        out_shape=jax.ShapeDtypeStruct((bsz, s, w), BF16),
        grid=(bsz, s // SB_TQ),
        in_specs=[
            pl.BlockSpec((1, SB_TQ, w), lambda b, i: (b, i, 0)),
            pl.BlockSpec((1, s, w), lambda b, i: (b, 0, 1)),
            pl.BlockSpec((1, s, w), lambda b, i: (b, 0, 2)),
            pl.BlockSpec((1, SB_TQ, w), lambda b, i: (b, i, 0)),
        ],
        out_specs=pl.BlockSpec((1, SB_TQ, w), lambda b, i: (b, i, 0)),
        scratch_shapes=[
            pltpu.VMEM((n_chains, rows, SB_GROUP_W), BF16),
            pltpu.VMEM((n_chains, rows, SB_GROUP_W), F32),
            pltpu.VMEM((n_chains, rows, 1), F32),
        ],
        compiler_params=pltpu.CompilerParams(
            dimension_semantics=("arbitrary", "arbitrary"),
            vmem_limit_bytes=V7X_VMEM_LIMIT_BYTES),
        name="stick_breaking",
    )(pa, pa, pa, pf)


def _hgrn_sum_matrix():
    c = HG_C
    w = np.zeros((HG_NMAT, c, c), np.float32)
    for lvl in range(1, HG_LEVELS + 1):
        n = 1 << lvl
        for r in range(c):
            m = (r // n) * n + n // 2
            if r >= m:
                w[lvl - 1, r, m:r + 1] = 1.0
            else:
                w[lvl - 1, r, r + 1:m] = 1.0
    for r in range(c):
        w[HG_LEVELS, r, :r + 1] = 1.0
        w[HG_LEVELS + 1, r, r + 1:] = 1.0
    return w.reshape(HG_NMAT * c, c)


def _hgrn_level_matrix():
    c = HG_C
    lv = np.full((c, c), -1, np.int32)
    for t in range(c):
        lv[t, t] = 0
        for s in range(t):
            lv[t, s] = (t ^ s).bit_length()
    return lv


def _hgrn_kernel(layer, q_ref, f_ref, i_ref, z_ref, lb_ref, nw_ref, w_ref, lv_ref, o_ref,
                 state_ref):
    c = HG_C
    dk = HG_HEAD_DIM
    ci = pl.program_id(1)

    @pl.when(ci == 0)
    def _():
        state_ref[...] = jnp.zeros_like(state_ref)

    lbw = lb_ref[...]
    e = jnp.exp(lbw - jnp.max(lbw, axis=0, keepdims=True))
    p = e / jnp.sum(e, axis=0, keepdims=True)
    lb_all = jnp.zeros((1, HG_WIDTH), F32)
    for l in range(1, layer + 1):
        lb_all = lb_all + p[l:l + 1]

    lv = lv_ref[...]
    f_all = lb_all + (1.0 - lb_all) * _sigmoid(f_ref[0])
    g_all = jnp.log(f_all) * LOG2E
    g_hi, g_lo = _split_bf16(g_all)
    for h in range(HG_HEADS):
        cols = slice(h * dk, (h + 1) * dk)
        if h % 2 == 0:
            pair = slice(h * dk, (h + 2) * dk)
            g_cat = jnp.concatenate([g_hi[:, pair], g_lo[:, pair]], axis=0)
            ex_pair = jnp.exp2(_dot(w_ref[...], g_cat))
            ex_pair_bf = ex_pair.astype(BF16)
        half = slice((h % 2) * dk, (h % 2 + 1) * dk)
        ex_bf = ex_pair_bf[:, half]
        k_bf = (1.0 - f_all[:, cols]).astype(BF16)
        q_bf = _silu(q_ref[0, :, cols]).astype(BF16)
        v_bf = i_ref[0, :, cols]

        scores = jnp.where(lv == 0, _nt_dot(q_bf, k_bf), 0.0)
        for lvl in range(1, HG_LEVELS + 1):
            dec = ex_bf[(lvl - 1) * c:lvl * c]
            scores = jnp.where(lv == lvl, _nt_dot(q_bf * dec, k_bf * dec), scores)

        dec_q = ex_bf[HG_LEVELS * c:(HG_LEVELS + 1) * c]
        dec_k = ex_bf[(HG_LEVELS + 1) * c:(HG_LEVELS + 2) * c]
        state = state_ref[h]
        inter = _nt_dot(q_bf * dec_q, state.astype(BF16))
        intra = _dot(scores.astype(BF16), v_bf)
        o = inter + intra

        v_t = v_bf.astype(F32).T.astype(BF16)
        dec_end = ex_pair[(HG_LEVELS + 1) * c - 1:(HG_LEVELS + 1) * c, half]
        state_ref[h] = state * dec_end + _dot(v_t, k_bf * dec_k)

        ms = jnp.mean(o * o, axis=-1, keepdims=True)
        o = o * lax.rsqrt(ms + RMS_EPS) * nw_ref[...]
        o_ref[0, :, cols] = (o * _silu(z_ref[0, :, cols])).astype(o_ref.dtype)


def _hgrn2(pa, pf, lower_bounds, norm_w_l, layer):
    bsz, s, _ = pa.shape
    depth = lower_bounds.shape[0]
    dk = HG_HEAD_DIM
    w = HG_WIDTH
    w_one = _hgrn_sum_matrix()
    w_sum = jnp.asarray(np.concatenate([w_one, w_one], axis=1), BF16)
    lv = jnp.asarray(_hgrn_level_matrix())
    return pl.pallas_call(
        partial(_hgrn_kernel, layer),
        out_shape=jax.ShapeDtypeStruct((bsz, s, w), BF16),
        grid=(bsz, s // HG_C),
        in_specs=[
            pl.BlockSpec((1, HG_C, w), lambda b, i: (b, i, 1)),
            pl.BlockSpec((1, HG_C, w), lambda b, i: (b, i, 2)),
            pl.BlockSpec((1, HG_C, w), lambda b, i: (b, i, 3)),
            pl.BlockSpec((1, HG_C, w), lambda b, i: (b, i, 3)),
            pl.BlockSpec((depth, w), lambda b, i: (0, 0)),
            pl.BlockSpec((1, dk), lambda b, i: (0, 0)),
            pl.BlockSpec((HG_NMAT * HG_C, 2 * HG_C), lambda b, i: (0, 0)),
            pl.BlockSpec((HG_C, HG_C), lambda b, i: (0, 0)),
        ],
        out_specs=pl.BlockSpec((1, HG_C, w), lambda b, i: (b, i, 0)),
        scratch_shapes=[pltpu.VMEM((HG_HEADS, dk, dk), F32)],
        compiler_params=pltpu.CompilerParams(
            dimension_semantics=("arbitrary", "arbitrary"),
            vmem_limit_bytes=V7X_VMEM_LIMIT_BYTES),
        name="hgrn2",
    )(pf, pf, pa, pf, lower_bounds, norm_w_l.reshape(1, dk), w_sum, lv)


def _merge_kernel(alpha, emit_h, ya_ref, yb_ref, pre_ref, post_ref, u_ref, zc_ref,
                  pre_h_ref, u_h_ref, ga_ref, gb_ref, gc_ref, x_ref, gate_ref, cw_ref,
                  wb_ref, wo_ref, lng_ref, lnb_ref, *rest):
    if emit_h:
        nmod_ref, xo_ref, h_ref, ext_ref = rest
    else:
        xo_ref, ext_ref = rest
    i = pl.program_id(1)
    tm = pre_ref.shape[1]

    halo = jnp.where(i > 0, pre_h_ref[0] * u_h_ref[0], 0.0)
    ext_ref[0:SUBLANES, :] = halo
    ext_ref[SUBLANES:SUBLANES + tm, :] = pre_ref[0] * u_ref[0]
    cw = cw_ref[...]
    conv = (cw[2:3] * ext_ref[SUBLANES:SUBLANES + tm, :]
            + cw[1:2] * ext_ref[SUBLANES - 1:SUBLANES - 1 + tm, :]
            + cw[0:1] * ext_ref[SUBLANES - 2:SUBLANES - 2 + tm, :])
    y_c = (post_ref[0] * conv * _silu(zc_ref[0])).astype(BF16)

    merged = (_sigmoid(ga_ref[0]) * _dot(ya_ref[0], wb_ref[0])
              + _sigmoid(gb_ref[0]) * _dot(yb_ref[0], wb_ref[1])
              + _sigmoid(gc_ref[0]) * _dot(y_c, wb_ref[2]))
    y = _dot(merged.astype(BF16), wo_ref[...])

    r = alpha * x_ref[0] + (1.0 + gate_ref[0]) * y
    x_new = _standardize(r) * lng_ref[...] + lnb_ref[...]
    xo_ref[0] = x_new
    if emit_h:
        d = x_new.shape[-1]
        shift = nmod_ref[0, :, 0:d]
        scale = nmod_ref[0, :, d:2 * d]
        h_ref[0] = (_standardize(x_new) * (1.0 + scale) + shift).astype(h_ref.dtype)


def _merge(ya, yb, pf, x, mod_l, conv_w_l, wb_l, wo_l, lng_l, lnb_l, alpha, next_mod):
    bsz, s, d = x.shape
    tm = 512
    cw = CONV_WIDTH
    emit_h = next_mod is not None
    halo_blocks = tm // SUBLANES

    def halo_map(col):
        return lambda b, i: (b, jnp.maximum(i * halo_blocks - 1, 0), col)

    in_specs = [
        pl.BlockSpec((1, tm, SB_WIDTH), lambda b, i: (b, i, 0)),
        pl.BlockSpec((1, tm, HG_WIDTH), lambda b, i: (b, i, 0)),
        pl.BlockSpec((1, tm, cw), lambda b, i: (b, i, 4)),
        pl.BlockSpec((1, tm, cw), lambda b, i: (b, i, 5)),
        pl.BlockSpec((1, tm, cw), lambda b, i: (b, i, 6)),
        pl.BlockSpec((1, tm, cw), lambda b, i: (b, i, 7)),
        pl.BlockSpec((1, SUBLANES, cw), halo_map(4)),
        pl.BlockSpec((1, SUBLANES, cw), halo_map(6)),
        pl.BlockSpec((1, tm, d), lambda b, i: (b, i, 4)),
        pl.BlockSpec((1, tm, d), lambda b, i: (b, i, 5)),
        pl.BlockSpec((1, tm, d), lambda b, i: (b, i, 6)),
        pl.BlockSpec((1, tm, d), lambda b, i: (b, i, 0)),
        pl.BlockSpec((1, 1, d), lambda b, i: (b, 0, 2)),
        pl.BlockSpec((CONV_K, cw), lambda b, i: (0, 0)),
        pl.BlockSpec((3, SB_WIDTH, d), lambda b, i: (0, 0, 0)),
        pl.BlockSpec((d, d), lambda b, i: (0, 0)),
        pl.BlockSpec((1, d), lambda b, i: (0, 0)),
        pl.BlockSpec((1, d), lambda b, i: (0, 0)),
    ]
    args = [ya, yb, pf, pf, pf, pf, pf, pf, pf, pf, pf, x, mod_l, conv_w_l, wb_l, wo_l,
            lng_l.reshape(1, d), lnb_l.reshape(1, d)]
    out_shape = [jax.ShapeDtypeStruct((bsz, s, d), F32)]
    out_specs = [pl.BlockSpec((1, tm, d), lambda b, i: (b, i, 0))]
    if emit_h:
        in_specs.append(pl.BlockSpec((1, 1, 3 * d), lambda b, i: (b, 0, 0)))
        args.append(next_mod)
        out_shape.append(jax.ShapeDtypeStruct((bsz, s, d), BF16))
        out_specs.append(pl.BlockSpec((1, tm, d), lambda b, i: (b, i, 0)))
    outs = pl.pallas_call(
        partial(_merge_kernel, alpha, emit_h),
        out_shape=out_shape,
        grid=(bsz, s // tm),
        in_specs=in_specs,
        out_specs=out_specs,
        scratch_shapes=[pltpu.VMEM((tm + SUBLANES, cw), F32)],
        compiler_params=pltpu.CompilerParams(
            dimension_semantics=("arbitrary", "arbitrary"),
            vmem_limit_bytes=V7X_VMEM_LIMIT_BYTES),
        name="merge_residual",
    )(*args)
    return (outs[0], outs[1]) if emit_h else (outs[0], None)


def _split_w_in(w_in_l):
    blk = [w_in_l[:, i * 512:(i + 1) * 512] for i in range(12)]
    gates = w_in_l[:, 12 * 512:]
    q_scale = LOG2E * SB_HEAD_DIM ** -0.5
    w_a = jnp.concatenate([blk[0] * q_scale, blk[1], blk[2], blk[6]], axis=1)
    w_f = jnp.concatenate([blk[3], blk[4], blk[5], blk[7], blk[8], blk[9], blk[10], blk[11],
                           gates], axis=1)
    return w_a.astype(BF16), w_f.astype(BF16)


def kernel(x, c, w_mod, b_mod, w_in, conv_w, hgrn_norm_w, lower_bounds, w_branch, w_out, ln_g, ln_b):
    bsz, s, d = x.shape
    depth = w_mod.shape[0]
    alpha = (2.0 * depth) ** 0.25

    mod = _modulation(c, w_mod, b_mod)
    mods = [mod[l].reshape(bsz, 1, 3 * d) for l in range(depth)]

    h = _lnmod(x, mods[0])
    for l in range(depth):
        w_a, w_f = _split_w_in(w_in[l])
        h2 = h.reshape(bsz * s, d)
        pa = _matmul(h2, w_a, BF16, 1024, 1024, "in_proj_bf16").reshape(bsz, s, PA_COLS)
        pf = _matmul(h2, w_f, F32, 1024, 1024, "in_proj_f32").reshape(bsz, s, PF_COLS)
        ya = _stick_breaking(pa, pf)
        yb = _hgrn2(pa, pf, lower_bounds, hgrn_norm_w[l], l)
        next_mod = mods[l + 1] if l + 1 < depth else None
        x, h = _merge(ya, yb, pf, x, mods[l], conv_w[l], w_branch[l].astype(BF16),
                      w_out[l].astype(BF16), ln_g[l], ln_b[l], alpha, next_mod)
    return x
```

```python
from functools import partial

import numpy as np
import jax
import jax.numpy as jnp
from jax import lax
from jax.experimental import pallas as pl
from jax.experimental.pallas import tpu as pltpu

D_MODEL = 1024
SB_HEADS = 8
SB_HEAD_DIM = 64
SB_WIDTH = SB_HEADS * SB_HEAD_DIM
HG_HEADS = 4
HG_HEAD_DIM = 128
HG_WIDTH = HG_HEADS * HG_HEAD_DIM
CONV_WIDTH = 512
CONV_K = 3
LN_EPS = 1e-5
RMS_EPS = 1e-6

F32 = jnp.float32
BF16 = jnp.bfloat16

V7X_VMEM_LIMIT_BYTES = 56 * 1024 * 1024
SUBLANES = 8
HALO_ROWS = 16

IN_COLS = 12 * 512 + 3 * D_MODEL
(COL_QA, COL_KA, COL_VA, COL_ZA, COL_QB, COL_FB, COL_IB, COL_ZB,
 COL_PRE, COL_POST, COL_U, COL_ZC) = range(12)
COL_GATES = 12 * 512 // D_MODEL
IN_PROJ_TM = 2048
IN_PROJ_TN = 1536

SB_GROUP = 4
SB_GROUP_W = SB_GROUP * SB_HEAD_DIM
SB_N_GROUPS = SB_HEADS // SB_GROUP
SB_SUB = 128
SB_TQ = 256
SB_TK = 256
SB_BPS = 2
SB_CHAINS = [(bb, sub, g) for bb in range(SB_BPS) for g in range(SB_N_GROUPS)
             for sub in range(SB_TQ // SB_SUB)]
LOG2E = 1.4426950408889634
LOG2E_BF16_HI = 1.4453125
LOG2E_BF16_LO = -0.00262451171875

HG_C = 128
HG_CPS = 4
HG_LEVELS = 7
HG_NMAT = HG_LEVELS + 2


def _nt_dot(a, b):
    return lax.dot_general(a, b, (((1,), (1,)), ((), ())), preferred_element_type=F32)


def _dot(a, b):
    return jnp.dot(a, b, preferred_element_type=F32)


def _split_bf16(v):
    hi = v.astype(BF16)
    lo = (v - hi.astype(F32)).astype(BF16)
    return hi, lo


def _sigmoid(v):
    return 1.0 / (1.0 + jnp.exp(-v))


def _silu(v):
    return v * _sigmoid(v)


def _standardize(xf):
    mu = jnp.mean(xf, axis=-1, keepdims=True)
    xc = xf - mu
    var = jnp.mean(xc * xc, axis=-1, keepdims=True)
    return xc * lax.rsqrt(var + LN_EPS)


def _mod_kernel(c_ref, w_ref, b_ref, o_ref):
    c_hi, c_lo = _split_bf16(c_ref[...])
    w_hi, w_lo = _split_bf16(w_ref[0])
    acc = _dot(c_hi, w_hi) + (_dot(c_hi, w_lo) + _dot(c_lo, w_hi))
    o_ref[0] = acc + b_ref[0]


def _modulation(c, w_mod, b_mod):
    depth, d, n = w_mod.shape
    bsz = c.shape[0]
    tn = 1024
    return pl.pallas_call(
        _mod_kernel,
        out_shape=jax.ShapeDtypeStruct((depth, bsz, n), F32),
        grid=(depth, n // tn),
        in_specs=[
            pl.BlockSpec((bsz, d), lambda l, j: (0, 0)),
            pl.BlockSpec((1, d, tn), lambda l, j: (l, 0, j)),
            pl.BlockSpec((1, 1, tn), lambda l, j: (l, 0, j)),
        ],
        out_specs=pl.BlockSpec((1, bsz, tn), lambda l, j: (l, 0, j)),
        compiler_params=pltpu.CompilerParams(
            dimension_semantics=("arbitrary", "arbitrary"),
            vmem_limit_bytes=V7X_VMEM_LIMIT_BYTES),
        name="adaln_mod",
    )(c, w_mod, b_mod.reshape(depth, 1, n))


def _lnmod_kernel(x_ref, mod_ref, h_ref):
    d = x_ref.shape[-1]
    shift = mod_ref[0, :, 0:d]
    scale = mod_ref[0, :, d:2 * d]
    h_ref[0] = (_standardize(x_ref[0]) * (1.0 + scale) + shift).astype(h_ref.dtype)


def _lnmod(x, mod_l):
    bsz, s, d = x.shape
    tm = 512
    return pl.pallas_call(
        _lnmod_kernel,
        out_shape=jax.ShapeDtypeStruct((bsz, s, d), BF16),
        grid=(bsz, s // tm),
        in_specs=[
            pl.BlockSpec((1, tm, d), lambda b, i: (b, i, 0)),
            pl.BlockSpec((1, 1, 3 * d), lambda b, i: (b, 0, 0)),
        ],
        out_specs=pl.BlockSpec((1, tm, d), lambda b, i: (b, i, 0)),
        compiler_params=pltpu.CompilerParams(
            dimension_semantics=("arbitrary", "arbitrary"),
            vmem_limit_bytes=V7X_VMEM_LIMIT_BYTES),
        name="ln_modulate",
    )(x, mod_l)


def _in_proj_kernel(h_ref, w_ref, cs_ref, o_ref, wbf_ref):
    @pl.when(pl.program_id(1) == 0)
    def _():
        wbf_ref[...] = (w_ref[...] * cs_ref[...]).astype(BF16)

    o_ref[...] = _dot(h_ref[...], wbf_ref[...]).astype(o_ref.dtype)


def _in_proj(h2, w_l, col_scale):
    m, k = h2.shape
    n = w_l.shape[1]
    tm, tn = IN_PROJ_TM, IN_PROJ_TN
    return pl.pallas_call(
        _in_proj_kernel,
        out_shape=jax.ShapeDtypeStruct((m, n), BF16),
        grid=(n // tn, m // tm),
        in_specs=[
            pl.BlockSpec((tm, k), lambda j, i: (i, 0)),
            pl.BlockSpec((k, tn), lambda j, i: (0, j)),
            pl.BlockSpec((1, tn), lambda j, i: (0, j)),
        ],
        out_specs=pl.BlockSpec((tm, tn), lambda j, i: (i, j)),
        scratch_shapes=[pltpu.VMEM((k, tn), BF16)],
        compiler_params=pltpu.CompilerParams(
            dimension_semantics=("arbitrary", "arbitrary"),
            vmem_limit_bytes=V7X_VMEM_LIMIT_BYTES),
        name="in_proj",
    )(h2, w_l, col_scale)


def _sb_diag_masks():
    rows = SB_GROUP * SB_SUB
    t_in = np.arange(rows)[:, None] % SB_SUB
    s_in = np.arange(SB_TK)[None, :]
    return np.stack([(s_in < sub * SB_SUB + t_in) for sub in range(SB_TQ // SB_SUB)]
                    ).astype(np.float32)


def _sb_kernel(q_ref, k_ref, v_ref, z_ref, dm_ref, o_ref, qs_ref, acc_ref, car_ref):
    qi = pl.program_id(1)
    sub_q, tk, gw = SB_SUB, SB_TK, SB_GROUP_W
    rows = SB_GROUP * sub_q

    lane_head = lax.broadcasted_iota(jnp.int32, (sub_q, gw), 1) // SB_HEAD_DIM
    for c, (bb, sub, g) in enumerate(SB_CHAINS):
        qf = q_ref[bb, sub * sub_q:(sub + 1) * sub_q, g * gw:(g + 1) * gw].astype(F32)
        qs_ref[c] = jnp.concatenate(
            [jnp.where(lane_head == h, qf, 0.0) for h in range(SB_GROUP)], axis=0).astype(BF16)

    jj = lax.broadcasted_iota(jnp.int32, (tk, tk), 0)
    ss = lax.broadcasted_iota(jnp.int32, (tk, tk), 1)
    u_mat = jnp.where(jj >= ss, 1.0, 0.0).astype(BF16)

    acc_ref[...] = jnp.zeros_like(acc_ref)
    car_ref[...] = jnp.zeros_like(car_ref)

    def block(kb, masked):
        start = pl.multiple_of(kb * tk, tk)
        zs, sps, rbs = [], [], []
        for c, (bb, sub, g) in enumerate(SB_CHAINS):
            kblk = k_ref[bb, pl.ds(start, tk), g * gw:(g + 1) * gw]
            zs.append(_nt_dot(qs_ref[c], kblk))
        for c, (bb, sub, g) in enumerate(SB_CHAINS):
            zb = zs[c].astype(BF16)
            ln_w = jnp.log(1.0 + jnp.exp2(-jnp.abs(zb)))
            sp = jnp.maximum(zb, 0.0) + (ln_w * LOG2E_BF16_HI + ln_w * LOG2E_BF16_LO)
            if masked:
                sp = sp * dm_ref[sub]
            sps.append(sp)
        for c in range(len(SB_CHAINS)):
            rbs.append(_dot(sps[c], u_mat))
        for c, (bb, sub, g) in enumerate(SB_CHAINS):
            vblk = v_ref[bb, pl.ds(start, tk), g * gw:(g + 1) * gw]
            arg = zs[c] - (rbs[c] + car_ref[c])
            if masked:
                a = jnp.exp2(jnp.minimum(arg, 0.0).astype(BF16)) * dm_ref[sub]
            else:
                a = jnp.exp2(arg.astype(BF16))
            acc_ref[c] += _dot(a, vblk)
            car_ref[c] += rbs[c][:, 0:1]

    block(qi, True)

    def body(i, carry):
        block(qi - 1 - i, False)
        return carry

    lax.fori_loop(0, qi, body, 0)

    for c, (bb, sub, g) in enumerate(SB_CHAINS):
        acc = acc_ref[c]
        o = jnp.zeros((sub_q, gw), F32)
        for h in range(SB_GROUP):
            o = o + jnp.where(lane_head == h, acc[h * sub_q:(h + 1) * sub_q], 0.0)
        zg = z_ref[bb, sub * sub_q:(sub + 1) * sub_q, g * gw:(g + 1) * gw].astype(F32)
        o_ref[bb, sub * sub_q:(sub + 1) * sub_q, g * gw:(g + 1) * gw] = (
            o * _silu(zg)).astype(o_ref.dtype)


def _stick_breaking(proj):
    bsz, s, _ = proj.shape
    w = SB_WIDTH
    n_chains = len(SB_CHAINS)
    rows = SB_GROUP * SB_SUB
    return pl.pallas_call(
        _sb_kernel,
        out_shape=jax.ShapeDtypeStruct((bsz, s, w), BF16),
        grid=(bsz // SB_BPS, s // SB_TQ),
        in_specs=[
            pl.BlockSpec((SB_BPS, SB_TQ, w), lambda b, i: (b, i, COL_QA)),
            pl.BlockSpec((SB_BPS, s, w), lambda b, i: (b, 0, COL_KA)),
            pl.BlockSpec((SB_BPS, s, w), lambda b, i: (b, 0, COL_VA)),
            pl.BlockSpec((SB_BPS, SB_TQ, w), lambda b, i: (b, i, COL_ZA)),
            pl.BlockSpec((SB_TQ // SB_SUB, rows, SB_TK), lambda b, i: (0, 0, 0)),
        ],
        out_specs=pl.BlockSpec((SB_BPS, SB_TQ, w), lambda b, i: (b, i, 0)),
        scratch_shapes=[
            pltpu.VMEM((n_chains, rows, SB_GROUP_W), BF16),
            pltpu.VMEM((n_chains, rows, SB_GROUP_W), F32),
            pltpu.VMEM((n_chains, rows, 1), F32),
        ],
        compiler_params=pltpu.CompilerParams(
            dimension_semantics=("arbitrary", "arbitrary"),
            vmem_limit_bytes=V7X_VMEM_LIMIT_BYTES),
        name="stick_breaking",
    )(proj, proj, proj, proj, jnp.asarray(_sb_diag_masks(), BF16))


def _hgrn_sum_matrix():
    c = HG_C
    w = np.zeros((HG_NMAT, c, c), np.float32)
    for lvl in range(1, HG_LEVELS + 1):
        n = 1 << lvl
        for r in range(c):
            m = (r // n) * n + n // 2
            if r >= m:
                w[lvl - 1, r, m:r + 1] = 1.0
            else:
                w[lvl - 1, r, r + 1:m] = 1.0
    for r in range(c):
        w[HG_LEVELS, r, :r + 1] = 1.0
        w[HG_LEVELS + 1, r, r + 1:] = 1.0
    return w.reshape(HG_NMAT * c, c)


def _hgrn_level_matrix():
    c = HG_C
    lv = np.full((c, c), -1, np.int32)
    for t in range(c):
        lv[t, t] = 0
        for s in range(t):
            lv[t, s] = (t ^ s).bit_length()
    return lv


def _hgrn_kernel(layer, q_ref, f_ref, i_ref, z_ref, lb_ref, nw_ref, w_ref, lv_ref, o_ref,
                 state_ref):
    c = HG_C
    dk = HG_HEAD_DIM
    ci = pl.program_id(1)

    @pl.when(ci == 0)
    def _():
        state_ref[...] = jnp.zeros_like(state_ref)

    lbw = lb_ref[...]
    e = jnp.exp(lbw - jnp.max(lbw, axis=0, keepdims=True))
    p = e / jnp.sum(e, axis=0, keepdims=True)
    lb_all = jnp.zeros((1, HG_WIDTH), F32)
    for l in range(1, layer + 1):
        lb_all = lb_all + p[l:l + 1]

    lv = lv_ref[...]
    for cc, h in [(cc, h) for cc in range(HG_CPS) for h in range(HG_HEADS)]:
        rws = slice(cc * c, (cc + 1) * c)
        cols = slice(h * dk, (h + 1) * dk)
        if h == 0:
            f_all = lb_all + (1.0 - lb_all) * _sigmoid(f_ref[0, rws, :].astype(F32))
            g_all = jnp.log(f_all) * LOG2E
            g_hi, g_lo = _split_bf16(g_all)
        if h % 2 == 0:
            pair = slice(h * dk, (h + 2) * dk)
            g_cat = jnp.concatenate([g_hi[:, pair], g_lo[:, pair]], axis=0)
            ex_pair = jnp.exp2(_dot(w_ref[...], g_cat))
            ex_pair_bf = ex_pair.astype(BF16)
        half = slice((h % 2) * dk, (h % 2 + 1) * dk)
        ex_bf = ex_pair_bf[:, half]
        k_bf = (1.0 - f_all[:, cols]).astype(BF16)
        q_bf = _silu(q_ref[0, rws, cols].astype(F32)).astype(BF16)
        v_bf = i_ref[0, rws, cols]

        scores = jnp.where(lv == 0, _nt_dot(q_bf, k_bf), 0.0)
        for lvl in range(1, HG_LEVELS + 1):
            dec = ex_bf[(lvl - 1) * c:lvl * c]
            scores = jnp.where(lv == lvl, _nt_dot(q_bf * dec, k_bf * dec), scores)

        dec_q = ex_bf[HG_LEVELS * c:(HG_LEVELS + 1) * c]
        dec_k = ex_bf[(HG_LEVELS + 1) * c:(HG_LEVELS + 2) * c]
        state = state_ref[h]
        inter = _nt_dot(q_bf * dec_q, state.astype(BF16))
        intra = _dot(scores.astype(BF16), v_bf)
        o = inter + intra

        v_t = v_bf.astype(F32).T.astype(BF16)
        dec_end = ex_pair[(HG_LEVELS + 1) * c - 1:(HG_LEVELS + 1) * c, half]
        state_ref[h] = state * dec_end + _dot(v_t, k_bf * dec_k)

        ms = jnp.mean(o * o, axis=-1, keepdims=True)
        o = o * lax.rsqrt(ms + RMS_EPS) * nw_ref[...]
        o_ref[0, rws, cols] = (o * _silu(z_ref[0, rws, cols].astype(F32))).astype(o_ref.dtype)


def _hgrn2(proj, lower_bounds, norm_w_l, layer):
    bsz, s, _ = proj.shape
    depth = lower_bounds.shape[0]
    dk = HG_HEAD_DIM
    w = HG_WIDTH
    w_one = _hgrn_sum_matrix()
    w_sum = jnp.asarray(np.concatenate([w_one, w_one], axis=1), BF16)
    lv = jnp.asarray(_hgrn_level_matrix())
    tr = HG_CPS * HG_C
    return pl.pallas_call(
        partial(_hgrn_kernel, layer),
        out_shape=jax.ShapeDtypeStruct((bsz, s, w), BF16),
        grid=(bsz, s // tr),
        in_specs=[
            pl.BlockSpec((1, tr, w), lambda b, i: (b, i, COL_QB)),
            pl.BlockSpec((1, tr, w), lambda b, i: (b, i, COL_FB)),
            pl.BlockSpec((1, tr, w), lambda b, i: (b, i, COL_IB)),
            pl.BlockSpec((1, tr, w), lambda b, i: (b, i, COL_ZB)),
            pl.BlockSpec((depth, w), lambda b, i: (0, 0)),
            pl.BlockSpec((1, dk), lambda b, i: (0, 0)),
            pl.BlockSpec((HG_NMAT * HG_C, 2 * HG_C), lambda b, i: (0, 0)),
            pl.BlockSpec((HG_C, HG_C), lambda b, i: (0, 0)),
        ],
        out_specs=pl.BlockSpec((1, tr, w), lambda b, i: (b, i, 0)),
        scratch_shapes=[pltpu.VMEM((HG_HEADS, dk, dk), F32)],
        compiler_params=pltpu.CompilerParams(
            dimension_semantics=("arbitrary", "arbitrary"),
            vmem_limit_bytes=V7X_VMEM_LIMIT_BYTES),
        name="hgrn2",
    )(proj, proj, proj, proj, lower_bounds, norm_w_l.reshape(1, dk), w_sum, lv)


def _merge_kernel(alpha, emit_h, ya_ref, yb_ref, pre_ref, post_ref, u_ref, zc_ref,
                  pre_h_ref, u_h_ref, ga_ref, gb_ref, gc_ref, x_ref, gate_ref, cw_ref,
                  wb_ref, wo_ref, lng_ref, lnb_ref, *rest):
    if emit_h:
        nmod_ref, xo_ref, h_ref, ext_ref = rest
    else:
        xo_ref, ext_ref = rest
    i = pl.program_id(1)
    tm = pre_ref.shape[1]

    hp = pre_h_ref[0, HALO_ROWS - SUBLANES:HALO_ROWS, :].astype(F32)
    hu = u_h_ref[0, HALO_ROWS - SUBLANES:HALO_ROWS, :].astype(F32)
    ext_ref[0:SUBLANES, :] = jnp.where(i > 0, hp * hu, 0.0)
    ext_ref[SUBLANES:SUBLANES + tm, :] = pre_ref[0].astype(F32) * u_ref[0].astype(F32)
    cw = cw_ref[...]
    conv = (cw[2:3] * ext_ref[SUBLANES:SUBLANES + tm, :]
            + cw[1:2] * ext_ref[SUBLANES - 1:SUBLANES - 1 + tm, :]
            + cw[0:1] * ext_ref[SUBLANES - 2:SUBLANES - 2 + tm, :])
    y_c = (post_ref[0].astype(F32) * conv * _silu(zc_ref[0].astype(F32))).astype(BF16)

    merged = (_sigmoid(ga_ref[0].astype(F32)) * _dot(ya_ref[0], wb_ref[0])
              + _sigmoid(gb_ref[0].astype(F32)) * _dot(yb_ref[0], wb_ref[1])
              + _sigmoid(gc_ref[0].astype(F32)) * _dot(y_c, wb_ref[2]))
    y = _dot(merged.astype(BF16), wo_ref[...])

    r = alpha * x_ref[0] + (1.0 + gate_ref[0]) * y
    x_new = _standardize(r) * lng_ref[...] + lnb_ref[...]
    xo_ref[0] = x_new
    if emit_h:
        d = x_new.shape[-1]
        shift = nmod_ref[0, :, 0:d]
        scale = nmod_ref[0, :, d:2 * d]
        h_ref[0] = (_standardize(x_new) * (1.0 + scale) + shift).astype(h_ref.dtype)


def _merge(ya, yb, proj, x, mod_l, conv_w_l, wb_l, wo_l, lng_l, lnb_l, alpha, next_mod):
    bsz, s, d = x.shape
    tm = 512
    cw = CONV_WIDTH
    emit_h = next_mod is not None
    halo_blocks = tm // HALO_ROWS

    def halo_map(col):
        return lambda b, i: (b, jnp.maximum(i * halo_blocks - 1, 0), col)

    in_specs = [
        pl.BlockSpec((1, tm, SB_WIDTH), lambda b, i: (b, i, 0)),
        pl.BlockSpec((1, tm, HG_WIDTH), lambda b, i: (b, i, 0)),
        pl.BlockSpec((1, tm, cw), lambda b, i: (b, i, COL_PRE)),
        pl.BlockSpec((1, tm, cw), lambda b, i: (b, i, COL_POST)),
        pl.BlockSpec((1, tm, cw), lambda b, i: (b, i, COL_U)),
        pl.BlockSpec((1, tm, cw), lambda b, i: (b, i, COL_ZC)),
        pl.BlockSpec((1, HALO_ROWS, cw), halo_map(COL_PRE)),
        pl.BlockSpec((1, HALO_ROWS, cw), halo_map(COL_U)),
        pl.BlockSpec((1, tm, d), lambda b, i: (b, i, COL_GATES)),
        pl.BlockSpec((1, tm, d), lambda b, i: (b, i, COL_GATES + 1)),
        pl.BlockSpec((1, tm, d), lambda b, i: (b, i, COL_GATES + 2)),
        pl.BlockSpec((1, tm, d), lambda b, i: (b, i, 0)),
        pl.BlockSpec((1, 1, d), lambda b, i: (b, 0, 2)),
        pl.BlockSpec((CONV_K, cw), lambda b, i: (0, 0)),
        pl.BlockSpec((3, SB_WIDTH, d), lambda b, i: (0, 0, 0)),
        pl.BlockSpec((d, d), lambda b, i: (0, 0)),
        pl.BlockSpec((1, d), lambda b, i: (0, 0)),
        pl.BlockSpec((1, d), lambda b, i: (0, 0)),
    ]
    args = [ya, yb] + [proj] * 9 + [x, mod_l, conv_w_l, wb_l, wo_l,
            lng_l.reshape(1, d), lnb_l.reshape(1, d)]
    out_shape = [jax.ShapeDtypeStruct((bsz, s, d), F32)]
    out_specs = [pl.BlockSpec((1, tm, d), lambda b, i: (b, i, 0))]
    if emit_h:
        in_specs.append(pl.BlockSpec((1, 1, 3 * d), lambda b, i: (b, 0, 0)))
        args.append(next_mod)
        out_shape.append(jax.ShapeDtypeStruct((bsz, s, d), BF16))
        out_specs.append(pl.BlockSpec((1, tm, d), lambda b, i: (b, i, 0)))
    outs = pl.pallas_call(
        partial(_merge_kernel, alpha, emit_h),
        out_shape=out_shape,
        grid=(bsz, s // tm),
        in_specs=in_specs,
        out_specs=out_specs,
        scratch_shapes=[pltpu.VMEM((tm + SUBLANES, cw), F32)],
        compiler_params=pltpu.CompilerParams(
            dimension_semantics=("arbitrary", "arbitrary"),
            vmem_limit_bytes=V7X_VMEM_LIMIT_BYTES),
        name="merge_residual",
    )(*args)
    return (outs[0], outs[1]) if emit_h else (outs[0], None)


def _in_proj_col_scale():
    cs = np.ones((1, IN_COLS), np.float32)
    cs[:, COL_QA * 512:(COL_QA + 1) * 512] = LOG2E * SB_HEAD_DIM ** -0.5
    return jnp.asarray(cs)


def kernel(x, c, w_mod, b_mod, w_in, conv_w, hgrn_norm_w, lower_bounds, w_branch, w_out, ln_g, ln_b):
    bsz, s, d = x.shape
    depth = w_mod.shape[0]
    alpha = (2.0 * depth) ** 0.25

    mod = _modulation(c, w_mod, b_mod).reshape(depth, bsz, 1, 3 * d)
    col_scale = _in_proj_col_scale()

    h = _lnmod(x, mod[0])
    for l in range(depth):
        proj = _in_proj(h.reshape(bsz * s, d), w_in[l], col_scale).reshape(bsz, s, IN_COLS)
        ya = _stick_breaking(proj)
        yb = _hgrn2(proj, lower_bounds, hgrn_norm_w[l], l)
        next_mod = mod[l + 1] if l + 1 < depth else None
        x, h = _merge(ya, yb, proj, x, mod[l], conv_w[l], w_branch[l].astype(BF16),
                      w_out[l].astype(BF16), ln_g[l], ln_b[l], alpha, next_mod)
    return x
```

```python
from functools import partial

import numpy as np
import jax
import jax.numpy as jnp
from jax import lax
from jax.experimental import pallas as pl
from jax.experimental.pallas import tpu as pltpu

D_MODEL = 1024
SB_HEADS = 8
SB_HEAD_DIM = 64
SB_WIDTH = SB_HEADS * SB_HEAD_DIM
HG_HEADS = 4
HG_HEAD_DIM = 128
HG_WIDTH = HG_HEADS * HG_HEAD_DIM
CONV_WIDTH = 512
CONV_K = 3
LN_EPS = 1e-5
RMS_EPS = 1e-6

F32 = jnp.float32
BF16 = jnp.bfloat16

V7X_VMEM_LIMIT_BYTES = 56 * 1024 * 1024
SUBLANES = 8
HALO_ROWS = 16

IN_COLS = 12 * 512 + 3 * D_MODEL
(COL_QA, COL_KA, COL_VA, COL_ZA, COL_QB, COL_FB, COL_IB, COL_ZB,
 COL_PRE, COL_POST, COL_U, COL_ZC) = range(12)
COL_GATES = 12 * 512 // D_MODEL
IN_PROJ_TM = 2048
IN_PROJ_TN = 1536
MERGE_TM = 512
LNMOD_TM = 1024

SB_GROUP = 4
SB_GROUP_W = SB_GROUP * SB_HEAD_DIM
SB_N_GROUPS = SB_HEADS // SB_GROUP
SB_SUB = 256
SB_TQ = 256
SB_TK = 256
SB_BPS = 2
SB_CHAINS = [(bb, sub, g) for bb in range(SB_BPS) for g in range(SB_N_GROUPS)
             for sub in range(SB_TQ // SB_SUB)]
LOG2E = 1.4426950408889634
LOG2E_BF16_HI = 1.4453125
LOG2E_BF16_LO = -0.00262451171875

HG_C = 128
HG_CPS = 4
HG_LEVELS = 7
HG_NMAT = HG_LEVELS + 2


def _nt_dot(a, b):
    return lax.dot_general(a, b, (((1,), (1,)), ((), ())), preferred_element_type=F32)


def _dot(a, b):
    return jnp.dot(a, b, preferred_element_type=F32)


def _split_bf16(v):
    hi = v.astype(BF16)
    lo = (v - hi.astype(F32)).astype(BF16)
    return hi, lo


def _sigmoid(v):
    return 1.0 / (1.0 + jnp.exp(-v))


def _silu(v):
    return v * _sigmoid(v)


def _standardize(xf):
    mu = jnp.mean(xf, axis=-1, keepdims=True)
    xc = xf - mu
    var = jnp.mean(xc * xc, axis=-1, keepdims=True)
    return xc * lax.rsqrt(var + LN_EPS)


def _mod_kernel(c_ref, w_ref, b_ref, o_ref):
    c_hi, c_lo = _split_bf16(c_ref[...])
    w_hi, w_lo = _split_bf16(w_ref[0])
    acc = _dot(c_hi, w_hi) + (_dot(c_hi, w_lo) + _dot(c_lo, w_hi))
    o_ref[0] = acc + b_ref[0]


def _modulation(c, w_mod, b_mod):
    depth, d, n = w_mod.shape
    bsz = c.shape[0]
    tn = 1024
    return pl.pallas_call(
        _mod_kernel,
        out_shape=jax.ShapeDtypeStruct((depth, bsz, n), F32),
        grid=(depth, n // tn),
        in_specs=[
            pl.BlockSpec((bsz, d), lambda l, j: (0, 0)),
            pl.BlockSpec((1, d, tn), lambda l, j: (l, 0, j)),
            pl.BlockSpec((1, 1, tn), lambda l, j: (l, 0, j)),
        ],
        out_specs=pl.BlockSpec((1, bsz, tn), lambda l, j: (l, 0, j)),
        compiler_params=pltpu.CompilerParams(
            dimension_semantics=("arbitrary", "arbitrary"),
            vmem_limit_bytes=V7X_VMEM_LIMIT_BYTES),
        name="adaln_mod",
    )(c, w_mod, b_mod.reshape(depth, 1, n))


def _lnmod_kernel(x_ref, mod_ref, h_ref):
    d = x_ref.shape[-1]
    shift = mod_ref[0, :, 0:d]
    scale = mod_ref[0, :, d:2 * d]
    h_ref[0] = (_standardize(x_ref[0]) * (1.0 + scale) + shift).astype(h_ref.dtype)


def _lnmod(x, mod_l):
    bsz, s, d = x.shape
    tm = LNMOD_TM
    return pl.pallas_call(
        _lnmod_kernel,
        out_shape=jax.ShapeDtypeStruct((bsz, s, d), BF16),
        grid=(bsz, s // tm),
        in_specs=[
            pl.BlockSpec((1, tm, d), lambda b, i: (b, i, 0)),
            pl.BlockSpec((1, 1, 3 * d), lambda b, i: (b, 0, 0)),
        ],
        out_specs=pl.BlockSpec((1, tm, d), lambda b, i: (b, i, 0)),
        compiler_params=pltpu.CompilerParams(
            dimension_semantics=("arbitrary", "arbitrary"),
            vmem_limit_bytes=V7X_VMEM_LIMIT_BYTES),
        name="ln_modulate",
    )(x, mod_l)


def _in_proj_kernel(h_ref, w_ref, cs_ref, o_ref, wbf_ref):
    @pl.when(pl.program_id(1) == 0)
    def _():
        wbf_ref[...] = (w_ref[0] * cs_ref[...]).astype(BF16)

    o_ref[...] = _dot(h_ref[...], wbf_ref[...]).astype(o_ref.dtype)


def _in_proj(h2, w_in, layer, col_scale):
    m, k = h2.shape
    n = w_in.shape[2]
    tm, tn = IN_PROJ_TM, IN_PROJ_TN
    return pl.pallas_call(
        _in_proj_kernel,
        out_shape=jax.ShapeDtypeStruct((m, n), BF16),
        grid=(n // tn, m // tm),
        in_specs=[
            pl.BlockSpec((tm, k), lambda j, i: (i, 0)),
            pl.BlockSpec((1, k, tn), lambda j, i: (layer, 0, j)),
            pl.BlockSpec((1, tn), lambda j, i: (0, j)),
        ],
        out_specs=pl.BlockSpec((tm, tn), lambda j, i: (i, j)),
        scratch_shapes=[pltpu.VMEM((k, tn), BF16)],
        compiler_params=pltpu.CompilerParams(
            dimension_semantics=("arbitrary", "arbitrary"),
            vmem_limit_bytes=V7X_VMEM_LIMIT_BYTES),
        name="in_proj",
    )(h2, w_in, col_scale)


def _sb_diag_masks():
    rows = SB_GROUP * SB_SUB
    t_in = np.arange(rows)[:, None] % SB_SUB
    s_in = np.arange(SB_TK)[None, :]
    return np.stack([(s_in < sub * SB_SUB + t_in) for sub in range(SB_TQ // SB_SUB)]
                    ).astype(np.float32)


def _sb_kernel(q_ref, k_ref, v_ref, z_ref, dm_ref, o_ref, qs_ref, acc_ref, car_ref):
    qi = pl.program_id(1)
    sub_q, tk, gw = SB_SUB, SB_TK, SB_GROUP_W
    rows = SB_GROUP * sub_q

    lane_head = lax.broadcasted_iota(jnp.int32, (sub_q, gw), 1) // SB_HEAD_DIM
    for c, (bb, sub, g) in enumerate(SB_CHAINS):
        qf = q_ref[bb, sub * sub_q:(sub + 1) * sub_q, g * gw:(g + 1) * gw].astype(F32)
        qs_ref[c] = jnp.concatenate(
            [jnp.where(lane_head == h, qf, 0.0) for h in range(SB_GROUP)], axis=0).astype(BF16)

    jj = lax.broadcasted_iota(jnp.int32, (tk, tk), 0)
    ss = lax.broadcasted_iota(jnp.int32, (tk, tk), 1)
    u_mat = jnp.where(jj >= ss, 1.0, 0.0).astype(BF16)

    acc_ref[...] = jnp.zeros_like(acc_ref)
    car_ref[...] = jnp.zeros_like(car_ref)

    def softplus2_bf16(z):
        zb = z.astype(BF16)
        ln_w = jnp.log(1.0 + jnp.exp2(-jnp.abs(zb)))
        return jnp.maximum(zb, 0.0) + (ln_w * LOG2E_BF16_HI + ln_w * LOG2E_BF16_LO)

    def block(kb, masked):
        start = pl.multiple_of(kb * tk, tk)
        for c, (bb, sub, g) in enumerate(SB_CHAINS):
            gcols = slice(g * gw, (g + 1) * gw)
            z = _nt_dot(qs_ref[c], k_ref[bb, pl.ds(start, tk), gcols])
            sp = softplus2_bf16(z)
            if masked:
                sp = sp * dm_ref[sub]
            rb = _dot(sp, u_mat)
            arg = z - (rb + car_ref[c])
            if masked:
                a = jnp.exp2(jnp.minimum(arg, 0.0).astype(BF16)) * dm_ref[sub]
            else:
                a = jnp.exp2(arg.astype(BF16))
            acc_ref[c] += _dot(a, v_ref[bb, pl.ds(start, tk), gcols])
            car_ref[c] += rb[:, 0:1]

    block(qi, True)

    def body(i, carry):
        block(qi - 1 - i, False)
        return carry

    lax.fori_loop(0, qi, body, 0)

    for c, (bb, sub, g) in enumerate(SB_CHAINS):
        acc = acc_ref[c]
        o = jnp.zeros((sub_q, gw), F32)
        for h in range(SB_GROUP):
            o = o + jnp.where(lane_head == h, acc[h * sub_q:(h + 1) * sub_q], 0.0)
        zg = z_ref[bb, sub * sub_q:(sub + 1) * sub_q, g * gw:(g + 1) * gw].astype(F32)
        o_ref[bb, sub * sub_q:(sub + 1) * sub_q, g * gw:(g + 1) * gw] = (
            o * _silu(zg)).astype(o_ref.dtype)


def _stick_breaking(proj):
    bsz, s, _ = proj.shape
    w = SB_WIDTH
    n_chains = len(SB_CHAINS)
    rows = SB_GROUP * SB_SUB
    return pl.pallas_call(
        _sb_kernel,
        out_shape=jax.ShapeDtypeStruct((bsz, s, w), BF16),
        grid=(bsz // SB_BPS, s // SB_TQ),
        in_specs=[
            pl.BlockSpec((SB_BPS, SB_TQ, w), lambda b, i: (b, i, COL_QA)),
            pl.BlockSpec((SB_BPS, s, w), lambda b, i: (b, 0, COL_KA)),
            pl.BlockSpec((SB_BPS, s, w), lambda b, i: (b, 0, COL_VA)),
            pl.BlockSpec((SB_BPS, SB_TQ, w), lambda b, i: (b, i, COL_ZA)),
            pl.BlockSpec((SB_TQ // SB_SUB, rows, SB_TK), lambda b, i: (0, 0, 0)),
        ],
        out_specs=pl.BlockSpec((SB_BPS, SB_TQ, w), lambda b, i: (b, i, 0)),
        scratch_shapes=[
            pltpu.VMEM((n_chains, rows, SB_GROUP_W), BF16),
            pltpu.VMEM((n_chains, rows, SB_GROUP_W), F32),
            pltpu.VMEM((n_chains, rows, 1), F32),
        ],
        compiler_params=pltpu.CompilerParams(
            dimension_semantics=("arbitrary", "arbitrary"),
            vmem_limit_bytes=V7X_VMEM_LIMIT_BYTES),
        name="stick_breaking",
    )(proj, proj, proj, proj, jnp.asarray(_sb_diag_masks(), BF16))


def _hgrn_sum_matrix():
    c = HG_C
    w = np.zeros((HG_NMAT, c, c), np.float32)
    for lvl in range(1, HG_LEVELS + 1):
        n = 1 << lvl
        for r in range(c):
            m = (r // n) * n + n // 2
            if r >= m:
                w[lvl - 1, r, m:r + 1] = 1.0
            else:
                w[lvl - 1, r, r + 1:m] = 1.0
    for r in range(c):
        w[HG_LEVELS, r, :r + 1] = 1.0
        w[HG_LEVELS + 1, r, r + 1:] = 1.0
    return w.reshape(HG_NMAT * c, c)


def _hgrn_level_matrix():
    c = HG_C
    lv = np.full((c, c), -1, np.int32)
    for t in range(c):
        lv[t, t] = 0
        for s in range(t):
            lv[t, s] = (t ^ s).bit_length()
    return lv


def _hgrn_kernel(layer, q_ref, f_ref, i_ref, z_ref, lb_ref, nw_ref, w_ref, lv_ref, o_ref,
                 state_ref):
    c = HG_C
    dk = HG_HEAD_DIM
    ci = pl.program_id(1)

    @pl.when(ci == 0)
    def _():
        state_ref[...] = jnp.zeros_like(state_ref)

    lbw = lb_ref[...]
    e = jnp.exp(lbw - jnp.max(lbw, axis=0, keepdims=True))
    p = e / jnp.sum(e, axis=0, keepdims=True)
    lb_all = jnp.zeros((1, HG_WIDTH), F32)
    for l in range(1, layer + 1):
        lb_all = lb_all + p[l:l + 1]

    lv = lv_ref[...]
    for cc, h in [(cc, h) for cc in range(HG_CPS) for h in range(HG_HEADS)]:
        rws = slice(cc * c, (cc + 1) * c)
        cols = slice(h * dk, (h + 1) * dk)
        if h == 0:
            f_all = lb_all + (1.0 - lb_all) * _sigmoid(f_ref[0, rws, :].astype(F32))
            g_all = jnp.log(f_all) * LOG2E
            g_hi, g_lo = _split_bf16(g_all)
        if h % 2 == 0:
            pair = slice(h * dk, (h + 2) * dk)
            g_cat = jnp.concatenate([g_hi[:, pair], g_lo[:, pair]], axis=0)
            ex_pair = jnp.exp2(_dot(w_ref[...], g_cat))
            ex_pair_bf = ex_pair.astype(BF16)
        half = slice((h % 2) * dk, (h % 2 + 1) * dk)
        ex_bf = ex_pair_bf[:, half]
        k_bf = (1.0 - f_all[:, cols]).astype(BF16)
        q_bf = _silu(q_ref[0, rws, cols].astype(F32)).astype(BF16)
        v_bf = i_ref[0, rws, cols]

        scores = jnp.where(lv == 0, _nt_dot(q_bf, k_bf), 0.0)
        for lvl in range(1, HG_LEVELS + 1):
            dec = ex_bf[(lvl - 1) * c:lvl * c]
            scores = jnp.where(lv == lvl, _nt_dot(q_bf * dec, k_bf * dec), scores)

        dec_q = ex_bf[HG_LEVELS * c:(HG_LEVELS + 1) * c]
        dec_k = ex_bf[(HG_LEVELS + 1) * c:(HG_LEVELS + 2) * c]
        state = state_ref[h]
        inter = _nt_dot(q_bf * dec_q, state.astype(BF16))
        intra = _dot(scores.astype(BF16), v_bf)
        o = inter + intra

        v_t = v_bf.astype(F32).T.astype(BF16)
        dec_end = ex_pair[(HG_LEVELS + 1) * c - 1:(HG_LEVELS + 1) * c, half]
        state_ref[h] = state * dec_end + _dot(v_t, k_bf * dec_k)

        ms = jnp.mean(o * o, axis=-1, keepdims=True)
        o = o * lax.rsqrt(ms + RMS_EPS) * nw_ref[...]
        o_ref[0, rws, cols] = (o * _silu(z_ref[0, rws, cols].astype(F32))).astype(o_ref.dtype)


def _hgrn2(proj, lower_bounds, norm_w_l, layer):
    bsz, s, _ = proj.shape
    depth = lower_bounds.shape[0]
    dk = HG_HEAD_DIM
    w = HG_WIDTH
    w_one = _hgrn_sum_matrix()
    w_sum = jnp.asarray(np.concatenate([w_one, w_one], axis=1), BF16)
    lv = jnp.asarray(_hgrn_level_matrix())
    tr = HG_CPS * HG_C
    return pl.pallas_call(
        partial(_hgrn_kernel, layer),
        out_shape=jax.ShapeDtypeStruct((bsz, s, w), BF16),
        grid=(bsz, s // tr),
        in_specs=[
            pl.BlockSpec((1, tr, w), lambda b, i: (b, i, COL_QB)),
            pl.BlockSpec((1, tr, w), lambda b, i: (b, i, COL_FB)),
            pl.BlockSpec((1, tr, w), lambda b, i: (b, i, COL_IB)),
            pl.BlockSpec((1, tr, w), lambda b, i: (b, i, COL_ZB)),
            pl.BlockSpec((depth, w), lambda b, i: (0, 0)),
            pl.BlockSpec((1, dk), lambda b, i: (0, 0)),
            pl.BlockSpec((HG_NMAT * HG_C, 2 * HG_C), lambda b, i: (0, 0)),
            pl.BlockSpec((HG_C, HG_C), lambda b, i: (0, 0)),
        ],
        out_specs=pl.BlockSpec((1, tr, w), lambda b, i: (b, i, 0)),
        scratch_shapes=[pltpu.VMEM((HG_HEADS, dk, dk), F32)],
        compiler_params=pltpu.CompilerParams(
            dimension_semantics=("arbitrary", "arbitrary"),
            vmem_limit_bytes=V7X_VMEM_LIMIT_BYTES),
        name="hgrn2",
    )(proj, proj, proj, proj, lower_bounds, norm_w_l.reshape(1, dk), w_sum, lv)


def _merge_kernel(alpha, emit_h, ya_ref, yb_ref, pre_ref, post_ref, u_ref, zc_ref,
                  pre_h_ref, u_h_ref, ga_ref, gb_ref, gc_ref, x_ref, gate_ref, cw_ref,
                  wb_ref, wo_ref, lng_ref, lnb_ref, *rest):
    if emit_h:
        nmod_ref, xo_ref, h_ref, ext_ref = rest
    else:
        xo_ref, ext_ref = rest
    i = pl.program_id(1)
    tm = pre_ref.shape[1]

    hp = pre_h_ref[0, HALO_ROWS - SUBLANES:HALO_ROWS, :].astype(F32)
    hu = u_h_ref[0, HALO_ROWS - SUBLANES:HALO_ROWS, :].astype(F32)
    ext_ref[0:SUBLANES, :] = jnp.where(i > 0, hp * hu, 0.0)
    ext_ref[SUBLANES:SUBLANES + tm, :] = pre_ref[0].astype(F32) * u_ref[0].astype(F32)
    cw = cw_ref[...]
    conv = (cw[2:3] * ext_ref[SUBLANES:SUBLANES + tm, :]
            + cw[1:2] * ext_ref[SUBLANES - 1:SUBLANES - 1 + tm, :]
            + cw[0:1] * ext_ref[SUBLANES - 2:SUBLANES - 2 + tm, :])
    y_c = (post_ref[0].astype(F32) * conv * _silu(zc_ref[0].astype(F32))).astype(BF16)

    merged = (_sigmoid(ga_ref[0].astype(F32)) * _dot(ya_ref[0], wb_ref[0])
              + _sigmoid(gb_ref[0].astype(F32)) * _dot(yb_ref[0], wb_ref[1])
              + _sigmoid(gc_ref[0].astype(F32)) * _dot(y_c, wb_ref[2]))
    y = _dot(merged.astype(BF16), wo_ref[...])

    r = alpha * x_ref[0] + (1.0 + gate_ref[0]) * y
    x_new = _standardize(r) * lng_ref[...] + lnb_ref[...]
    xo_ref[0] = x_new
    if emit_h:
        d = x_new.shape[-1]
        shift = nmod_ref[0, :, 0:d]
        scale = nmod_ref[0, :, d:2 * d]
        h_ref[0] = (_standardize(x_new) * (1.0 + scale) + shift).astype(h_ref.dtype)


def _merge(ya, yb, proj, x, mod_l, conv_w_l, wb_l, wo_l, lng_l, lnb_l, alpha, next_mod):
    bsz, s, d = x.shape
    tm = MERGE_TM
    cw = CONV_WIDTH
    emit_h = next_mod is not None
    halo_blocks = tm // HALO_ROWS

    def halo_map(col):
        return lambda b, i: (b, jnp.maximum(i * halo_blocks - 1, 0), col)

    in_specs = [
        pl.BlockSpec((1, tm, SB_WIDTH), lambda b, i: (b, i, 0)),
        pl.BlockSpec((1, tm, HG_WIDTH), lambda b, i: (b, i, 0)),
        pl.BlockSpec((1, tm, cw), lambda b, i: (b, i, COL_PRE)),
        pl.BlockSpec((1, tm, cw), lambda b, i: (b, i, COL_POST)),
        pl.BlockSpec((1, tm, cw), lambda b, i: (b, i, COL_U)),
        pl.BlockSpec((1, tm, cw), lambda b, i: (b, i, COL_ZC)),
        pl.BlockSpec((1, HALO_ROWS, cw), halo_map(COL_PRE)),
        pl.BlockSpec((1, HALO_ROWS, cw), halo_map(COL_U)),
        pl.BlockSpec((1, tm, d), lambda b, i: (b, i, COL_GATES)),
        pl.BlockSpec((1, tm, d), lambda b, i: (b, i, COL_GATES + 1)),
        pl.BlockSpec((1, tm, d), lambda b, i: (b, i, COL_GATES + 2)),
        pl.BlockSpec((1, tm, d), lambda b, i: (b, i, 0)),
        pl.BlockSpec((1, 1, d), lambda b, i: (b, 0, 2)),
        pl.BlockSpec((CONV_K, cw), lambda b, i: (0, 0)),
        pl.BlockSpec((3, SB_WIDTH, d), lambda b, i: (0, 0, 0)),
        pl.BlockSpec((d, d), lambda b, i: (0, 0)),
        pl.BlockSpec((1, d), lambda b, i: (0, 0)),
        pl.BlockSpec((1, d), lambda b, i: (0, 0)),
    ]
    args = [ya, yb] + [proj] * 9 + [x, mod_l, conv_w_l, wb_l, wo_l,
            lng_l.reshape(1, d), lnb_l.reshape(1, d)]
    out_shape = [jax.ShapeDtypeStruct((bsz, s, d), F32)]
    out_specs = [pl.BlockSpec((1, tm, d), lambda b, i: (b, i, 0))]
    if emit_h:
        in_specs.append(pl.BlockSpec((1, 1, 3 * d), lambda b, i: (b, 0, 0)))
        args.append(next_mod)
        out_shape.append(jax.ShapeDtypeStruct((bsz, s, d), BF16))
        out_specs.append(pl.BlockSpec((1, tm, d), lambda b, i: (b, i, 0)))
    outs = pl.pallas_call(
        partial(_merge_kernel, alpha, emit_h),
        out_shape=out_shape,
        grid=(bsz, s // tm),
        in_specs=in_specs,
        out_specs=out_specs,
        scratch_shapes=[pltpu.VMEM((tm + SUBLANES, cw), F32)],
        compiler_params=pltpu.CompilerParams(
            dimension_semantics=("arbitrary", "arbitrary"),
            vmem_limit_bytes=V7X_VMEM_LIMIT_BYTES),
        name="merge_residual",
    )(*args)
    return (outs[0], outs[1]) if emit_h else (outs[0], None)


def _in_proj_col_scale():
    cs = np.ones((1, IN_COLS), np.float32)
    cs[:, COL_QA * 512:(COL_QA + 1) * 512] = LOG2E * SB_HEAD_DIM ** -0.5
    return jnp.asarray(cs)


def kernel(x, c, w_mod, b_mod, w_in, conv_w, hgrn_norm_w, lower_bounds, w_branch, w_out, ln_g, ln_b):
    bsz, s, d = x.shape
    depth = w_mod.shape[0]
    alpha = (2.0 * depth) ** 0.25

    mod = _modulation(c, w_mod, b_mod).reshape(depth, bsz, 1, 3 * d)
    col_scale = _in_proj_col_scale()

    h = _lnmod(x, mod[0])
    for l in range(depth):
        proj = _in_proj(h.reshape(bsz * s, d), w_in, l, col_scale).reshape(bsz, s, IN_COLS)
        ya = _stick_breaking(proj)
        yb = _hgrn2(proj, lower_bounds, hgrn_norm_w[l], l)
        next_mod = mod[l + 1] if l + 1 < depth else None
        x, h = _merge(ya, yb, proj, x, mod[l], conv_w[l], w_branch[l].astype(BF16),
                      w_out[l].astype(BF16), ln_g[l], ln_b[l], alpha, next_mod)
    return x
```

```python
from functools import partial

import numpy as np
import jax
import jax.numpy as jnp
from jax import lax
from jax.experimental import pallas as pl
from jax.experimental.pallas import tpu as pltpu

D_MODEL = 1024
SB_HEADS = 8
SB_HEAD_DIM = 64
SB_WIDTH = SB_HEADS * SB_HEAD_DIM
HG_HEADS = 4
HG_HEAD_DIM = 128
HG_WIDTH = HG_HEADS * HG_HEAD_DIM
CONV_WIDTH = 512
CONV_K = 3
LN_EPS = 1e-5
RMS_EPS = 1e-6

F32 = jnp.float32
BF16 = jnp.bfloat16

V7X_VMEM_LIMIT_BYTES = 56 * 1024 * 1024
SUBLANES = 8
HALO_ROWS = 16

IN_COLS = 12 * 512 + 3 * D_MODEL
(COL_QA, COL_KA, COL_VA, COL_ZA, COL_QB, COL_FB, COL_IB, COL_ZB,
 COL_PRE, COL_POST, COL_U, COL_ZC) = range(12)
COL_GATES = 12 * 512 // D_MODEL
IN_PROJ_TM = 2048
IN_PROJ_TN = 1536
MERGE_TM = 512
LNMOD_TM = 1024

SB_GROUP = 4
SB_GROUP_W = SB_GROUP * SB_HEAD_DIM
SB_N_GROUPS = SB_HEADS // SB_GROUP
SB_TQ = 512
SB_TK = 256
SB_N_DIAG = SB_TQ // SB_TK
SB_PART_ROWS = SB_GROUP * SB_TK
SB_BPS = 2
SB_CHAINS = [(bb, g) for bb in range(SB_BPS) for g in range(SB_N_GROUPS)]
LOG2E = 1.4426950408889634
LOG2E_BF16_HI = 1.4453125
LOG2E_BF16_LO = -0.00262451171875

HG_C = 128
HG_CPS = 4
HG_LEVELS = 7
HG_NMAT = HG_LEVELS + 2


def _nt_dot(a, b):
    return lax.dot_general(a, b, (((1,), (1,)), ((), ())), preferred_element_type=F32)


def _dot(a, b):
    return jnp.dot(a, b, preferred_element_type=F32)


def _split_bf16(v):
    hi = v.astype(BF16)
    lo = (v - hi.astype(F32)).astype(BF16)
    return hi, lo


def _sigmoid(v):
    return 1.0 / (1.0 + jnp.exp(-v))


def _silu(v):
    return v * _sigmoid(v)


def _standardize(xf):
    mu = jnp.mean(xf, axis=-1, keepdims=True)
    xc = xf - mu
    var = jnp.mean(xc * xc, axis=-1, keepdims=True)
    return xc * lax.rsqrt(var + LN_EPS)


def _mod_kernel(c_ref, w_ref, b_ref, o_ref):
    c_hi, c_lo = _split_bf16(c_ref[...])
    w_hi, w_lo = _split_bf16(w_ref[0])
    acc = _dot(c_hi, w_hi) + (_dot(c_hi, w_lo) + _dot(c_lo, w_hi))
    o_ref[0] = acc + b_ref[0]


def _modulation(c, w_mod, b_mod):
    depth, d, n = w_mod.shape
    bsz = c.shape[0]
    tn = 1024
    return pl.pallas_call(
        _mod_kernel,
        out_shape=jax.ShapeDtypeStruct((depth, bsz, n), F32),
        grid=(depth, n // tn),
        in_specs=[
            pl.BlockSpec((bsz, d), lambda l, j: (0, 0)),
            pl.BlockSpec((1, d, tn), lambda l, j: (l, 0, j)),
            pl.BlockSpec((1, 1, tn), lambda l, j: (l, 0, j)),
        ],
        out_specs=pl.BlockSpec((1, bsz, tn), lambda l, j: (l, 0, j)),
        compiler_params=pltpu.CompilerParams(
            dimension_semantics=("arbitrary", "arbitrary"),
            vmem_limit_bytes=V7X_VMEM_LIMIT_BYTES),
        name="adaln_mod",
    )(c, w_mod, b_mod.reshape(depth, 1, n))


def _lnmod_kernel(x_ref, mod_ref, h_ref):
    d = x_ref.shape[-1]
    shift = mod_ref[0, :, 0:d]
    scale = mod_ref[0, :, d:2 * d]
    h_ref[0] = (_standardize(x_ref[0]) * (1.0 + scale) + shift).astype(h_ref.dtype)


def _lnmod(x, mod_l):
    bsz, s, d = x.shape
    tm = LNMOD_TM
    return pl.pallas_call(
        _lnmod_kernel,
        out_shape=jax.ShapeDtypeStruct((bsz, s, d), BF16),
        grid=(bsz, s // tm),
        in_specs=[
            pl.BlockSpec((1, tm, d), lambda b, i: (b, i, 0)),
            pl.BlockSpec((1, 1, 3 * d), lambda b, i: (b, 0, 0)),
        ],
        out_specs=pl.BlockSpec((1, tm, d), lambda b, i: (b, i, 0)),
        compiler_params=pltpu.CompilerParams(
            dimension_semantics=("arbitrary", "arbitrary"),
            vmem_limit_bytes=V7X_VMEM_LIMIT_BYTES),
        name="ln_modulate",
    )(x, mod_l)


def _in_proj_kernel(h_ref, w_ref, cs_ref, o_ref, wbf_ref):
    @pl.when(pl.program_id(1) == 0)
    def _():
        wbf_ref[...] = (w_ref[0] * cs_ref[...]).astype(BF16)

    o_ref[...] = _dot(h_ref[...], wbf_ref[...]).astype(o_ref.dtype)


def _in_proj(h2, w_in, layer, col_scale):
    m, k = h2.shape
    n = w_in.shape[2]
    tm, tn = IN_PROJ_TM, IN_PROJ_TN
    return pl.pallas_call(
        _in_proj_kernel,
        out_shape=jax.ShapeDtypeStruct((m, n), BF16),
        grid=(n // tn, m // tm),
        in_specs=[
            pl.BlockSpec((tm, k), lambda j, i: (i, 0)),
            pl.BlockSpec((1, k, tn), lambda j, i: (layer, 0, j)),
            pl.BlockSpec((1, tn), lambda j, i: (0, j)),
        ],
        out_specs=pl.BlockSpec((tm, tn), lambda j, i: (i, j)),
        scratch_shapes=[pltpu.VMEM((k, tn), BF16)],
        compiler_params=pltpu.CompilerParams(
            dimension_semantics=("arbitrary", "arbitrary"),
            vmem_limit_bytes=V7X_VMEM_LIMIT_BYTES),
        name="in_proj",
    )(h2, w_in, col_scale)


def _sb_band_mask():
    t_in = np.arange(SB_PART_ROWS)[:, None] % SB_TK
    tri = np.arange(SB_TK)[None, :] < t_in
    ones = np.ones(((SB_N_DIAG - 1) * SB_PART_ROWS, SB_TK), bool)
    return np.concatenate([tri, ones], axis=0).astype(np.float32)


def _sb_kernel(q_ref, k_ref, v_ref, z_ref, dm_ref, o_ref, qs_ref, acc_ref, car_ref):
    qi = pl.program_id(1)
    tk, gw = SB_TK, SB_GROUP_W
    rows = SB_N_DIAG * SB_PART_ROWS

    lane_head = lax.broadcasted_iota(jnp.int32, (tk, gw), 1) // SB_HEAD_DIM
    for c, (bb, g) in enumerate(SB_CHAINS):
        for p in range(SB_N_DIAG):
            qf = q_ref[bb, p * tk:(p + 1) * tk, g * gw:(g + 1) * gw].astype(F32)
            qs_ref[c, p * SB_PART_ROWS:(p + 1) * SB_PART_ROWS, :] = jnp.concatenate(
                [jnp.where(lane_head == h, qf, 0.0) for h in range(SB_GROUP)],
                axis=0).astype(BF16)

    jj = lax.broadcasted_iota(jnp.int32, (tk, tk), 0)
    ss = lax.broadcasted_iota(jnp.int32, (tk, tk), 1)
    u_mat = jnp.where(jj >= ss, 1.0, 0.0).astype(BF16)

    acc_ref[...] = jnp.zeros_like(acc_ref)
    car_ref[...] = jnp.zeros_like(car_ref)

    def softplus2_bf16(z):
        zb = z.astype(BF16)
        ln_w = jnp.log(1.0 + jnp.exp2(-jnp.abs(zb)))
        return jnp.maximum(zb, 0.0) + (ln_w * LOG2E_BF16_HI + ln_w * LOG2E_BF16_LO)

    def block(kb, band):
        start = pl.multiple_of(kb * tk, tk)
        r0 = 0 if band is None else band * SB_PART_ROWS
        rws = slice(r0, rows)
        for c, (bb, g) in enumerate(SB_CHAINS):
            gcols = slice(g * gw, (g + 1) * gw)
            z = _nt_dot(qs_ref[c, rws, :], k_ref[bb, pl.ds(start, tk), gcols])
            sp = softplus2_bf16(z)
            if band is not None:
                sp = sp * dm_ref[0:rows - r0, :]
            rb = _dot(sp, u_mat)
            arg = z - (rb + car_ref[c, rws, :])
            if band is not None:
                a = jnp.exp2(jnp.minimum(arg, 0.0).astype(BF16)) * dm_ref[0:rows - r0, :]
            else:
                a = jnp.exp2(arg.astype(BF16))
            acc_ref[c, rws, :] += _dot(a, v_ref[bb, pl.ds(start, tk), gcols])
            car_ref[c, rws, :] += rb[:, 0:1]

    first_kb = qi * SB_N_DIAG
    for d in reversed(range(SB_N_DIAG)):
        block(first_kb + d, d)

    def body(i, carry):
        block(first_kb - 1 - i, None)
        return carry

    lax.fori_loop(0, first_kb, body, 0)

    for c, (bb, g) in enumerate(SB_CHAINS):
        for p in range(SB_N_DIAG):
            o = jnp.zeros((tk, gw), F32)
            for h in range(SB_GROUP):
                r = p * SB_PART_ROWS + h * tk
                o = o + jnp.where(lane_head == h, acc_ref[c, r:r + tk, :], 0.0)
            zg = z_ref[bb, p * tk:(p + 1) * tk, g * gw:(g + 1) * gw].astype(F32)
            o_ref[bb, p * tk:(p + 1) * tk, g * gw:(g + 1) * gw] = (
                o * _silu(zg)).astype(o_ref.dtype)


def _stick_breaking(proj):
    bsz, s, _ = proj.shape
    w = SB_WIDTH
    n_chains = len(SB_CHAINS)
    rows = SB_N_DIAG * SB_PART_ROWS
    return pl.pallas_call(
        _sb_kernel,
        out_shape=jax.ShapeDtypeStruct((bsz, s, w), BF16),
        grid=(bsz // SB_BPS, s // SB_TQ),
        in_specs=[
            pl.BlockSpec((SB_BPS, SB_TQ, w), lambda b, i: (b, i, COL_QA)),
            pl.BlockSpec((SB_BPS, s, w), lambda b, i: (b, 0, COL_KA)),
            pl.BlockSpec((SB_BPS, s, w), lambda b, i: (b, 0, COL_VA)),
            pl.BlockSpec((SB_BPS, SB_TQ, w), lambda b, i: (b, i, COL_ZA)),
            pl.BlockSpec((rows, SB_TK), lambda b, i: (0, 0)),
        ],
        out_specs=pl.BlockSpec((SB_BPS, SB_TQ, w), lambda b, i: (b, i, 0)),
        scratch_shapes=[
            pltpu.VMEM((n_chains, rows, SB_GROUP_W), BF16),
            pltpu.VMEM((n_chains, rows, SB_GROUP_W), F32),
            pltpu.VMEM((n_chains, rows, 1), F32),
        ],
        compiler_params=pltpu.CompilerParams(
            dimension_semantics=("arbitrary", "arbitrary"),
            vmem_limit_bytes=V7X_VMEM_LIMIT_BYTES),
        name="stick_breaking",
    )(proj, proj, proj, proj, jnp.asarray(_sb_band_mask(), BF16))


def _hgrn_sum_matrix():
    c = HG_C
    w = np.zeros((HG_NMAT, c, c), np.float32)
    for lvl in range(1, HG_LEVELS + 1):
        n = 1 << lvl
        for r in range(c):
            m = (r // n) * n + n // 2
            if r >= m:
                w[lvl - 1, r, m:r + 1] = 1.0
            else:
                w[lvl - 1, r, r + 1:m] = 1.0
    for r in range(c):
        w[HG_LEVELS, r, :r + 1] = 1.0
        w[HG_LEVELS + 1, r, r + 1:] = 1.0
    return w.reshape(HG_NMAT * c, c)


def _hgrn_level_matrix():
    c = HG_C
    lv = np.full((c, c), -1, np.int32)
    for t in range(c):
        lv[t, t] = 0
        for s in range(t):
            lv[t, s] = (t ^ s).bit_length()
    return lv


def _hgrn_kernel(layer, q_ref, f_ref, i_ref, z_ref, lb_ref, nw_ref, w_ref, lv_ref, o_ref,
                 state_ref):
    c = HG_C
    dk = HG_HEAD_DIM
    ci = pl.program_id(1)

    @pl.when(ci == 0)
    def _():
        state_ref[...] = jnp.zeros_like(state_ref)

    lbw = lb_ref[...]
    e = jnp.exp(lbw - jnp.max(lbw, axis=0, keepdims=True))
    p = e / jnp.sum(e, axis=0, keepdims=True)
    lb_all = jnp.zeros((1, HG_WIDTH), F32)
    for l in range(1, layer + 1):
        lb_all = lb_all + p[l:l + 1]

    lv = lv_ref[...]
    for cc, h in [(cc, h) for cc in range(HG_CPS) for h in range(HG_HEADS)]:
        rws = slice(cc * c, (cc + 1) * c)
        cols = slice(h * dk, (h + 1) * dk)
        if h == 0:
            f_all = lb_all + (1.0 - lb_all) * _sigmoid(f_ref[0, rws, :].astype(F32))
            g_all = jnp.log(f_all) * LOG2E
            g_hi, g_lo = _split_bf16(g_all)
        if h % 2 == 0:
            pair = slice(h * dk, (h + 2) * dk)
            g_cat = jnp.concatenate([g_hi[:, pair], g_lo[:, pair]], axis=0)
            ex_pair = jnp.exp2(_dot(w_ref[...], g_cat))
            ex_pair_bf = ex_pair.astype(BF16)
        half = slice((h % 2) * dk, (h % 2 + 1) * dk)
        ex_bf = ex_pair_bf[:, half]
        k_bf = (1.0 - f_all[:, cols]).astype(BF16)
        q_bf = _silu(q_ref[0, rws, cols].astype(F32)).astype(BF16)
        v_bf = i_ref[0, rws, cols]

        scores = jnp.where(lv == 0, _nt_dot(q_bf, k_bf), 0.0)
        for lvl in range(1, HG_LEVELS + 1):
            dec = ex_bf[(lvl - 1) * c:lvl * c]
            scores = jnp.where(lv == lvl, _nt_dot(q_bf * dec, k_bf * dec), scores)

        dec_q = ex_bf[HG_LEVELS * c:(HG_LEVELS + 1) * c]
        dec_k = ex_bf[(HG_LEVELS + 1) * c:(HG_LEVELS + 2) * c]
        state = state_ref[h]
        inter = _nt_dot(q_bf * dec_q, state.astype(BF16))
        intra = _dot(scores.astype(BF16), v_bf)
        o = inter + intra

        v_t = v_bf.astype(F32).T.astype(BF16)
        dec_end = ex_pair[(HG_LEVELS + 1) * c - 1:(HG_LEVELS + 1) * c, half]
        state_ref[h] = state * dec_end + _dot(v_t, k_bf * dec_k)

        ms = jnp.mean(o * o, axis=-1, keepdims=True)
        o = o * lax.rsqrt(ms + RMS_EPS) * nw_ref[...]
        o_ref[0, rws, cols] = (o * _silu(z_ref[0, rws, cols].astype(F32))).astype(o_ref.dtype)


def _hgrn2(proj, lower_bounds, norm_w_l, layer):
    bsz, s, _ = proj.shape
    depth = lower_bounds.shape[0]
    dk = HG_HEAD_DIM
    w = HG_WIDTH
    w_one = _hgrn_sum_matrix()
    w_sum = jnp.asarray(np.concatenate([w_one, w_one], axis=1), BF16)
    lv = jnp.asarray(_hgrn_level_matrix())
    tr = HG_CPS * HG_C
    return pl.pallas_call(
        partial(_hgrn_kernel, layer),
        out_shape=jax.ShapeDtypeStruct((bsz, s, w), BF16),
        grid=(bsz, s // tr),
        in_specs=[
            pl.BlockSpec((1, tr, w), lambda b, i: (b, i, COL_QB)),
            pl.BlockSpec((1, tr, w), lambda b, i: (b, i, COL_FB)),
            pl.BlockSpec((1, tr, w), lambda b, i: (b, i, COL_IB)),
            pl.BlockSpec((1, tr, w), lambda b, i: (b, i, COL_ZB)),
            pl.BlockSpec((depth, w), lambda b, i: (0, 0)),
            pl.BlockSpec((1, dk), lambda b, i: (0, 0)),
            pl.BlockSpec((HG_NMAT * HG_C, 2 * HG_C), lambda b, i: (0, 0)),
            pl.BlockSpec((HG_C, HG_C), lambda b, i: (0, 0)),
        ],
        out_specs=pl.BlockSpec((1, tr, w), lambda b, i: (b, i, 0)),
        scratch_shapes=[pltpu.VMEM((HG_HEADS, dk, dk), F32)],
        compiler_params=pltpu.CompilerParams(
            dimension_semantics=("arbitrary", "arbitrary"),
            vmem_limit_bytes=V7X_VMEM_LIMIT_BYTES),
        name="hgrn2",
    )(proj, proj, proj, proj, lower_bounds, norm_w_l.reshape(1, dk), w_sum, lv)


def _merge_kernel(alpha, emit_h, ya_ref, yb_ref, pre_ref, post_ref, u_ref, zc_ref,
                  pre_h_ref, u_h_ref, ga_ref, gb_ref, gc_ref, x_ref, gate_ref, cw_ref,
                  wb_ref, wo_ref, lng_ref, lnb_ref, *rest):
    if emit_h:
        nmod_ref, xo_ref, h_ref, ext_ref = rest
    else:
        xo_ref, ext_ref = rest
    i = pl.program_id(1)
    tm = pre_ref.shape[1]

    hp = pre_h_ref[0, HALO_ROWS - SUBLANES:HALO_ROWS, :].astype(F32)
    hu = u_h_ref[0, HALO_ROWS - SUBLANES:HALO_ROWS, :].astype(F32)
    ext_ref[0:SUBLANES, :] = jnp.where(i > 0, hp * hu, 0.0)
    ext_ref[SUBLANES:SUBLANES + tm, :] = pre_ref[0].astype(F32) * u_ref[0].astype(F32)
    cw = cw_ref[...]
    conv = (cw[2:3] * ext_ref[SUBLANES:SUBLANES + tm, :]
            + cw[1:2] * ext_ref[SUBLANES - 1:SUBLANES - 1 + tm, :]
            + cw[0:1] * ext_ref[SUBLANES - 2:SUBLANES - 2 + tm, :])
    y_c = (post_ref[0].astype(F32) * conv * _silu(zc_ref[0].astype(F32))).astype(BF16)

    merged = (_sigmoid(ga_ref[0].astype(F32)) * _dot(ya_ref[0], wb_ref[0])
              + _sigmoid(gb_ref[0].astype(F32)) * _dot(yb_ref[0], wb_ref[1])
              + _sigmoid(gc_ref[0].astype(F32)) * _dot(y_c, wb_ref[2]))
    y = _dot(merged.astype(BF16), wo_ref[...])

    r = alpha * x_ref[0] + (1.0 + gate_ref[0]) * y
    x_new = _standardize(r) * lng_ref[...] + lnb_ref[...]
    xo_ref[0] = x_new
    if emit_h:
        d = x_new.shape[-1]
        shift = nmod_ref[0, :, 0:d]
        scale = nmod_ref[0, :, d:2 * d]
        h_ref[0] = (_standardize(x_new) * (1.0 + scale) + shift).astype(h_ref.dtype)


def _merge(ya, yb, proj, x, mod_l, conv_w_l, wb_l, wo_l, lng_l, lnb_l, alpha, next_mod):
    bsz, s, d = x.shape
    tm = MERGE_TM
    cw = CONV_WIDTH
    emit_h = next_mod is not None
    halo_blocks = tm // HALO_ROWS

    def halo_map(col):
        return lambda b, i: (b, jnp.maximum(i * halo_blocks - 1, 0), col)

    in_specs = [
        pl.BlockSpec((1, tm, SB_WIDTH), lambda b, i: (b, i, 0)),
        pl.BlockSpec((1, tm, HG_WIDTH), lambda b, i: (b, i, 0)),
        pl.BlockSpec((1, tm, cw), lambda b, i: (b, i, COL_PRE)),
        pl.BlockSpec((1, tm, cw), lambda b, i: (b, i, COL_POST)),
        pl.BlockSpec((1, tm, cw), lambda b, i: (b, i, COL_U)),
        pl.BlockSpec((1, tm, cw), lambda b, i: (b, i, COL_ZC)),
        pl.BlockSpec((1, HALO_ROWS, cw), halo_map(COL_PRE)),
        pl.BlockSpec((1, HALO_ROWS, cw), halo_map(COL_U)),
        pl.BlockSpec((1, tm, d), lambda b, i: (b, i, COL_GATES)),
        pl.BlockSpec((1, tm, d), lambda b, i: (b, i, COL_GATES + 1)),
        pl.BlockSpec((1, tm, d), lambda b, i: (b, i, COL_GATES + 2)),
        pl.BlockSpec((1, tm, d), lambda b, i: (b, i, 0)),
        pl.BlockSpec((1, 1, d), lambda b, i: (b, 0, 2)),
        pl.BlockSpec((CONV_K, cw), lambda b, i: (0, 0)),
        pl.BlockSpec((3, SB_WIDTH, d), lambda b, i: (0, 0, 0)),
        pl.BlockSpec((d, d), lambda b, i: (0, 0)),
        pl.BlockSpec((1, d), lambda b, i: (0, 0)),
        pl.BlockSpec((1, d), lambda b, i: (0, 0)),
    ]
    args = [ya, yb] + [proj] * 9 + [x, mod_l, conv_w_l, wb_l, wo_l,
            lng_l.reshape(1, d), lnb_l.reshape(1, d)]
    out_shape = [jax.ShapeDtypeStruct((bsz, s, d), F32)]
    out_specs = [pl.BlockSpec((1, tm, d), lambda b, i: (b, i, 0))]
    if emit_h:
        in_specs.append(pl.BlockSpec((1, 1, 3 * d), lambda b, i: (b, 0, 0)))
        args.append(next_mod)
        out_shape.append(jax.ShapeDtypeStruct((bsz, s, d), BF16))
        out_specs.append(pl.BlockSpec((1, tm, d), lambda b, i: (b, i, 0)))
    outs = pl.pallas_call(
        partial(_merge_kernel, alpha, emit_h),
        out_shape=out_shape,
        grid=(bsz, s // tm),
        in_specs=in_specs,
        out_specs=out_specs,
        scratch_shapes=[pltpu.VMEM((tm + SUBLANES, cw), F32)],
        compiler_params=pltpu.CompilerParams(
            dimension_semantics=("arbitrary", "arbitrary"),
            vmem_limit_bytes=V7X_VMEM_LIMIT_BYTES),
        name="merge_residual",
    )(*args)
    return (outs[0], outs[1]) if emit_h else (outs[0], None)


def _in_proj_col_scale():
    cs = np.ones((1, IN_COLS), np.float32)
    cs[:, COL_QA * 512:(COL_QA + 1) * 512] = LOG2E * SB_HEAD_DIM ** -0.5
    return jnp.asarray(cs)


def kernel(x, c, w_mod, b_mod, w_in, conv_w, hgrn_norm_w, lower_bounds, w_branch, w_out, ln_g, ln_b):
    bsz, s, d = x.shape
    depth = w_mod.shape[0]
    alpha = (2.0 * depth) ** 0.25

    mod = _modulation(c, w_mod, b_mod).reshape(depth, bsz, 1, 3 * d)
    col_scale = _in_proj_col_scale()

    h = _lnmod(x, mod[0])
    for l in range(depth):
        proj = _in_proj(h.reshape(bsz * s, d), w_in, l, col_scale).reshape(bsz, s, IN_COLS)
        ya = _stick_breaking(proj)
        yb = _hgrn2(proj, lower_bounds, hgrn_norm_w[l], l)
        next_mod = mod[l + 1] if l + 1 < depth else None
        x, h = _merge(ya, yb, proj, x, mod[l], conv_w[l], w_branch[l].astype(BF16),
                      w_out[l].astype(BF16), ln_g[l], ln_b[l], alpha, next_mod)
    return x
```

```python
from functools import partial

import numpy as np
import jax
import jax.numpy as jnp
from jax import lax
from jax.experimental import pallas as pl
from jax.experimental.pallas import tpu as pltpu

D_MODEL = 1024
SB_HEADS = 8
SB_HEAD_DIM = 64
SB_WIDTH = SB_HEADS * SB_HEAD_DIM
HG_HEADS = 4
HG_HEAD_DIM = 128
HG_WIDTH = HG_HEADS * HG_HEAD_DIM
CONV_WIDTH = 512
CONV_K = 3
LN_EPS = 1e-5
RMS_EPS = 1e-6

F32 = jnp.float32
BF16 = jnp.bfloat16

V7X_VMEM_LIMIT_BYTES = 56 * 1024 * 1024
SUBLANES = 8
HALO_ROWS = 16

IN_COLS = 12 * 512 + 3 * D_MODEL
(COL_QA, COL_KA, COL_VA, COL_ZA, COL_QB, COL_FB, COL_IB, COL_ZB,
 COL_PRE, COL_POST, COL_U, COL_ZC) = range(12)
COL_GATES = 12 * 512 // D_MODEL
IN_PROJ_TM = 2048
IN_PROJ_TN = 1536
MERGE_TM = 512
LNMOD_TM = 1024

SB_GROUP = 4
SB_GROUP_W = SB_GROUP * SB_HEAD_DIM
SB_N_GROUPS = SB_HEADS // SB_GROUP
SB_TQ = 512
SB_TK = 256
SB_N_DIAG = SB_TQ // SB_TK
SB_PART_ROWS = SB_GROUP * SB_TK
SB_BPS = 2
SB_CHAINS = [(bb, g) for bb in range(SB_BPS) for g in range(SB_N_GROUPS)]
LOG2E = 1.4426950408889634
LOG2E_BF16_HI = 1.4453125
LOG2E_BF16_LO = -0.00262451171875

HG_C = 128
HG_CPS = 4
HG_LEVELS = 7
HG_BAND = 3
HG_MM_LEVELS = range(3, HG_LEVELS + 1)
HG_NMAT = len(HG_MM_LEVELS) + 2


def _nt_dot(a, b):
    return lax.dot_general(a, b, (((1,), (1,)), ((), ())), preferred_element_type=F32)


def _dot(a, b):
    return jnp.dot(a, b, preferred_element_type=F32)


def _split_bf16(v):
    hi = v.astype(BF16)
    lo = (v - hi.astype(F32)).astype(BF16)
    return hi, lo


def _sigmoid(v):
    return 1.0 / (1.0 + jnp.exp(-v))


def _silu(v):
    return v * _sigmoid(v)


def _standardize(xf):
    mu = jnp.mean(xf, axis=-1, keepdims=True)
    xc = xf - mu
    var = jnp.mean(xc * xc, axis=-1, keepdims=True)
    return xc * lax.rsqrt(var + LN_EPS)


def _mod_kernel(c_ref, w_ref, b_ref, o_ref):
    c_hi, c_lo = _split_bf16(c_ref[...])
    w_hi, w_lo = _split_bf16(w_ref[0])
    acc = _dot(c_hi, w_hi) + (_dot(c_hi, w_lo) + _dot(c_lo, w_hi))
    o_ref[0] = acc + b_ref[0]


def _modulation(c, w_mod, b_mod):
    depth, d, n = w_mod.shape
    bsz = c.shape[0]
    tn = 1024
    return pl.pallas_call(
        _mod_kernel,
        out_shape=jax.ShapeDtypeStruct((depth, bsz, n), F32),
        grid=(depth, n // tn),
        in_specs=[
            pl.BlockSpec((bsz, d), lambda l, j: (0, 0)),
            pl.BlockSpec((1, d, tn), lambda l, j: (l, 0, j)),
            pl.BlockSpec((1, 1, tn), lambda l, j: (l, 0, j)),
        ],
        out_specs=pl.BlockSpec((1, bsz, tn), lambda l, j: (l, 0, j)),
        compiler_params=pltpu.CompilerParams(
            dimension_semantics=("arbitrary", "arbitrary"),
            vmem_limit_bytes=V7X_VMEM_LIMIT_BYTES),
        name="adaln_mod",
    )(c, w_mod, b_mod.reshape(depth, 1, n))


def _lnmod_kernel(x_ref, mod_ref, h_ref):
    d = x_ref.shape[-1]
    shift = mod_ref[0, :, 0:d]
    scale = mod_ref[0, :, d:2 * d]
    h_ref[0] = (_standardize(x_ref[0]) * (1.0 + scale) + shift).astype(h_ref.dtype)


def _lnmod(x, mod_l):
    bsz, s, d = x.shape
    tm = LNMOD_TM
    return pl.pallas_call(
        _lnmod_kernel,
        out_shape=jax.ShapeDtypeStruct((bsz, s, d), BF16),
        grid=(bsz, s // tm),
        in_specs=[
            pl.BlockSpec((1, tm, d), lambda b, i: (b, i, 0)),
            pl.BlockSpec((1, 1, 3 * d), lambda b, i: (b, 0, 0)),
        ],
        out_specs=pl.BlockSpec((1, tm, d), lambda b, i: (b, i, 0)),
        compiler_params=pltpu.CompilerParams(
            dimension_semantics=("arbitrary", "arbitrary"),
            vmem_limit_bytes=V7X_VMEM_LIMIT_BYTES),
        name="ln_modulate",
    )(x, mod_l)


def _in_proj_kernel(h_ref, w_ref, cs_ref, o_ref, wbf_ref):
    @pl.when(pl.program_id(1) == 0)
    def _():
        wbf_ref[...] = (w_ref[0] * cs_ref[...]).astype(BF16)

    o_ref[...] = _dot(h_ref[...], wbf_ref[...]).astype(o_ref.dtype)


def _in_proj(h2, w_in, layer, col_scale):
    m, k = h2.shape
    n = w_in.shape[2]
    tm, tn = IN_PROJ_TM, IN_PROJ_TN
    return pl.pallas_call(
        _in_proj_kernel,
        out_shape=jax.ShapeDtypeStruct((m, n), BF16),
        grid=(n // tn, m // tm),
        in_specs=[
            pl.BlockSpec((tm, k), lambda j, i: (i, 0)),
            pl.BlockSpec((1, k, tn), lambda j, i: (layer, 0, j)),
            pl.BlockSpec((1, tn), lambda j, i: (0, j)),
        ],
        out_specs=pl.BlockSpec((tm, tn), lambda j, i: (i, j)),
        scratch_shapes=[pltpu.VMEM((k, tn), BF16)],
        compiler_params=pltpu.CompilerParams(
            dimension_semantics=("arbitrary", "arbitrary"),
            vmem_limit_bytes=V7X_VMEM_LIMIT_BYTES),
        name="in_proj",
    )(h2, w_in, col_scale)


def _sb_band_mask():
    t_in = np.arange(SB_PART_ROWS)[:, None] % SB_TK
    tri = np.arange(SB_TK)[None, :] < t_in
    ones = np.ones(((SB_N_DIAG - 1) * SB_PART_ROWS, SB_TK), bool)
    return np.concatenate([tri, ones], axis=0).astype(np.float32)


def _sb_kernel(q_ref, k_ref, v_ref, z_ref, dm_ref, o_ref, qs_ref, acc_ref, car_ref):
    qi = pl.program_id(1)
    tk, gw = SB_TK, SB_GROUP_W
    rows = SB_N_DIAG * SB_PART_ROWS

    lane_head = lax.broadcasted_iota(jnp.int32, (tk, gw), 1) // SB_HEAD_DIM
    for c, (bb, g) in enumerate(SB_CHAINS):
        for p in range(SB_N_DIAG):
            qf = q_ref[bb, p * tk:(p + 1) * tk, g * gw:(g + 1) * gw].astype(F32)
            qs_ref[c, p * SB_PART_ROWS:(p + 1) * SB_PART_ROWS, :] = jnp.concatenate(
                [jnp.where(lane_head == h, qf, 0.0) for h in range(SB_GROUP)],
                axis=0).astype(BF16)

    jj = lax.broadcasted_iota(jnp.int32, (tk, tk), 0)
    ss = lax.broadcasted_iota(jnp.int32, (tk, tk), 1)
    u_mat = jnp.where(jj >= ss, 1.0, 0.0).astype(BF16)

    car_ref[...] = jnp.zeros_like(car_ref)

    def softplus2_bf16(z):
        zb = z.astype(BF16)
        ln_w = jnp.log(1.0 + jnp.exp2(-jnp.abs(zb)))
        return jnp.maximum(zb, 0.0) + (ln_w * LOG2E_BF16_HI + ln_w * LOG2E_BF16_LO)

    def block(kb, band):
        start = pl.multiple_of(kb * tk, tk)
        r0 = 0 if band is None else band * SB_PART_ROWS
        rws = slice(r0, rows)
        for c, (bb, g) in enumerate(SB_CHAINS):
            gcols = slice(g * gw, (g + 1) * gw)
            z = _nt_dot(qs_ref[c, rws, :], k_ref[bb, pl.ds(start, tk), gcols])
            sp = softplus2_bf16(z)
            if band is not None:
                sp = sp * dm_ref[0:rows - r0, :]
            rb = _dot(sp, u_mat)
            arg = z - (rb + car_ref[c, rws, :])
            if band is not None:
                a = jnp.exp2(jnp.minimum(arg, 0.0).astype(BF16)) * dm_ref[0:rows - r0, :]
            else:
                a = jnp.exp2(arg.astype(BF16))
            av = _dot(a, v_ref[bb, pl.ds(start, tk), gcols])
            if band is None:
                acc_ref[c, rws, :] += av
            else:
                r1 = r0 + SB_PART_ROWS
                acc_ref[c, r0:r1, :] = av[0:SB_PART_ROWS]
                if r1 < rows:
                    acc_ref[c, r1:rows, :] += av[SB_PART_ROWS:]
            car_ref[c, rws, :] += rb[:, 0:1]

    first_kb = qi * SB_N_DIAG
    for d in reversed(range(SB_N_DIAG)):
        block(first_kb + d, d)

    def body(i, carry):
        block(first_kb - 1 - i, None)
        return carry

    lax.fori_loop(0, first_kb, body, 0)

    for c, (bb, g) in enumerate(SB_CHAINS):
        for p in range(SB_N_DIAG):
            o = jnp.zeros((tk, gw), F32)
            for h in range(SB_GROUP):
                r = p * SB_PART_ROWS + h * tk
                o = o + jnp.where(lane_head == h, acc_ref[c, r:r + tk, :], 0.0)
            zg = z_ref[bb, p * tk:(p + 1) * tk, g * gw:(g + 1) * gw].astype(F32)
            o_ref[bb, p * tk:(p + 1) * tk, g * gw:(g + 1) * gw] = (
                o * _silu(zg)).astype(o_ref.dtype)


def _stick_breaking(proj):
    bsz, s, _ = proj.shape
    w = SB_WIDTH
    n_chains = len(SB_CHAINS)
    rows = SB_N_DIAG * SB_PART_ROWS
    return pl.pallas_call(
        _sb_kernel,
        out_shape=jax.ShapeDtypeStruct((bsz, s, w), BF16),
        grid=(bsz // SB_BPS, s // SB_TQ),
        in_specs=[
            pl.BlockSpec((SB_BPS, SB_TQ, w), lambda b, i: (b, i, COL_QA)),
            pl.BlockSpec((SB_BPS, s, w), lambda b, i: (b, 0, COL_KA)),
            pl.BlockSpec((SB_BPS, s, w), lambda b, i: (b, 0, COL_VA)),
            pl.BlockSpec((SB_BPS, SB_TQ, w), lambda b, i: (b, i, COL_ZA)),
            pl.BlockSpec((rows, SB_TK), lambda b, i: (0, 0)),
        ],
        out_specs=pl.BlockSpec((SB_BPS, SB_TQ, w), lambda b, i: (b, i, 0)),
        scratch_shapes=[
            pltpu.VMEM((n_chains, rows, SB_GROUP_W), BF16),
            pltpu.VMEM((n_chains, rows, SB_GROUP_W), F32),
            pltpu.VMEM((n_chains, rows, 1), F32),
        ],
        compiler_params=pltpu.CompilerParams(
            dimension_semantics=("arbitrary", "arbitrary"),
            vmem_limit_bytes=V7X_VMEM_LIMIT_BYTES),
        name="stick_breaking",
    )(proj, proj, proj, proj, jnp.asarray(_sb_band_mask(), BF16))


def _hgrn_sum_matrix():
    c = HG_C
    w = np.zeros((HG_NMAT, c, c), np.float32)
    for i, lvl in enumerate(HG_MM_LEVELS):
        n = 1 << lvl
        for r in range(c):
            m = (r // n) * n + n // 2
            if r >= m:
                w[i, r, m:r + 1] = 1.0
            else:
                w[i, r, r + 1:m] = 1.0
    for r in range(c):
        w[HG_NMAT - 2, r, :r + 1] = 1.0
        w[HG_NMAT - 1, r, r + 1:] = 1.0
    return w.reshape(HG_NMAT * c, c)


def _hgrn_level_matrix():
    c = HG_C
    lv = np.full((c, c), -1, np.int32)
    for t in range(c):
        for s in range(t + 1):
            lv[t, s] = t - s if t - s <= HG_BAND else HG_BAND + (t ^ s).bit_length()
    return lv


def _hgrn_kernel(layer, q_ref, f_ref, i_ref, z_ref, lb_ref, nw_ref, w_ref, lv_ref, o_ref,
                 state_ref):
    c = HG_C
    dk = HG_HEAD_DIM
    ci = pl.program_id(1)

    @pl.when(ci == 0)
    def _():
        state_ref[...] = jnp.zeros_like(state_ref)

    lbw = lb_ref[...]
    e = jnp.exp(lbw - jnp.max(lbw, axis=0, keepdims=True))
    p = e / jnp.sum(e, axis=0, keepdims=True)
    lb_all = jnp.zeros((1, HG_WIDTH), F32)
    for l in range(1, layer + 1):
        lb_all = lb_all + p[l:l + 1]

    lv = lv_ref[...]
    for cc, h in [(cc, h) for cc in range(HG_CPS) for h in range(HG_HEADS)]:
        rws = slice(cc * c, (cc + 1) * c)
        cols = slice(h * dk, (h + 1) * dk)
        if h == 0:
            f_all = lb_all + (1.0 - lb_all) * _sigmoid(f_ref[0, rws, :].astype(F32))
            g_all = jnp.log(f_all) * LOG2E
            g_hi, g_lo = _split_bf16(g_all)
        if h % 2 == 0:
            pair = slice(h * dk, (h + 2) * dk)
            g_cat = jnp.concatenate([g_hi[:, pair], g_lo[:, pair]], axis=0)
            ex_pair = jnp.exp2(_dot(w_ref[...], g_cat))
            ex_pair_bf = ex_pair.astype(BF16)
        half = slice((h % 2) * dk, (h % 2 + 1) * dk)
        ex_bf = ex_pair_bf[:, half]
        f_h = f_all[:, cols]
        k_f = 1.0 - f_h
        q_f = _silu(q_ref[0, rws, cols].astype(F32))
        k_bf = k_f.astype(BF16)
        q_bf = q_f.astype(BF16)
        v_bf = i_ref[0, rws, cols]

        scores = jnp.where(lv == 0, jnp.sum(q_f * k_f, axis=-1, keepdims=True), 0.0)
        qd = q_f
        for j in range(1, HG_BAND + 1):
            qd = qd * (f_h if j == 1 else pltpu.roll(f_h, j - 1, axis=0))
            d_j = jnp.sum(qd * pltpu.roll(k_f, j, axis=0), axis=-1, keepdims=True)
            scores = jnp.where(lv == j, d_j, scores)
        for i, lvl in enumerate(HG_MM_LEVELS):
            dec = ex_bf[i * c:(i + 1) * c]
            scores = jnp.where(lv == HG_BAND + lvl, _nt_dot(q_bf * dec, k_bf * dec), scores)

        dec_q = ex_bf[(HG_NMAT - 2) * c:(HG_NMAT - 1) * c]
        dec_k = ex_bf[(HG_NMAT - 1) * c:HG_NMAT * c]
        state = state_ref[h]
        inter = _nt_dot(q_bf * dec_q, state.astype(BF16))
        intra = _dot(scores.astype(BF16), v_bf)
        o = inter + intra

        v_t = v_bf.astype(F32).T.astype(BF16)
        dec_end = ex_pair[(HG_NMAT - 1) * c - 1:(HG_NMAT - 1) * c, half]
        state_ref[h] = state * dec_end + _dot(v_t, k_bf * dec_k)

        ms = jnp.mean(o * o, axis=-1, keepdims=True)
        o = o * lax.rsqrt(ms + RMS_EPS) * nw_ref[...]
        o_ref[0, rws, cols] = (o * _silu(z_ref[0, rws, cols].astype(F32))).astype(o_ref.dtype)


def _hgrn2(proj, lower_bounds, norm_w_l, layer):
    bsz, s, _ = proj.shape
    depth = lower_bounds.shape[0]
    dk = HG_HEAD_DIM
    w = HG_WIDTH
    w_one = _hgrn_sum_matrix()
    w_sum = jnp.asarray(np.concatenate([w_one, w_one], axis=1), BF16)
    lv = jnp.asarray(_hgrn_level_matrix())
    tr = HG_CPS * HG_C
    return pl.pallas_call(
        partial(_hgrn_kernel, layer),
        out_shape=jax.ShapeDtypeStruct((bsz, s, w), BF16),
        grid=(bsz, s // tr),
        in_specs=[
            pl.BlockSpec((1, tr, w), lambda b, i: (b, i, COL_QB)),
            pl.BlockSpec((1, tr, w), lambda b, i: (b, i, COL_FB)),
            pl.BlockSpec((1, tr, w), lambda b, i: (b, i, COL_IB)),
            pl.BlockSpec((1, tr, w), lambda b, i: (b, i, COL_ZB)),
            pl.BlockSpec((depth, w), lambda b, i: (0, 0)),
            pl.BlockSpec((1, dk), lambda b, i: (0, 0)),
            pl.BlockSpec((HG_NMAT * HG_C, 2 * HG_C), lambda b, i: (0, 0)),
            pl.BlockSpec((HG_C, HG_C), lambda b, i: (0, 0)),
        ],
        out_specs=pl.BlockSpec((1, tr, w), lambda b, i: (b, i, 0)),
        scratch_shapes=[pltpu.VMEM((HG_HEADS, dk, dk), F32)],
        compiler_params=pltpu.CompilerParams(
            dimension_semantics=("arbitrary", "arbitrary"),
            vmem_limit_bytes=V7X_VMEM_LIMIT_BYTES),
        name="hgrn2",
    )(proj, proj, proj, proj, lower_bounds, norm_w_l.reshape(1, dk), w_sum, lv)


def _merge_kernel(alpha, emit_h, ya_ref, yb_ref, pre_ref, post_ref, u_ref, zc_ref,
                  pre_h_ref, u_h_ref, ga_ref, gb_ref, gc_ref, x_ref, gate_ref, cw_ref,
                  wb_ref, wo_ref, lng_ref, lnb_ref, *rest):
    if emit_h:
        nmod_ref, xo_ref, h_ref, ext_ref = rest
    else:
        xo_ref, ext_ref = rest
    i = pl.program_id(1)
    tm = pre_ref.shape[1]

    hp = pre_h_ref[0, HALO_ROWS - SUBLANES:HALO_ROWS, :].astype(F32)
    hu = u_h_ref[0, HALO_ROWS - SUBLANES:HALO_ROWS, :].astype(F32)
    ext_ref[0:SUBLANES, :] = jnp.where(i > 0, hp * hu, 0.0)
    ext_ref[SUBLANES:SUBLANES + tm, :] = pre_ref[0].astype(F32) * u_ref[0].astype(F32)
    cw = cw_ref[...]
    conv = (cw[2:3] * ext_ref[SUBLANES:SUBLANES + tm, :]
            + cw[1:2] * ext_ref[SUBLANES - 1:SUBLANES - 1 + tm, :]
            + cw[0:1] * ext_ref[SUBLANES - 2:SUBLANES - 2 + tm, :])
    y_c = (post_ref[0].astype(F32) * conv * _silu(zc_ref[0].astype(F32))).astype(BF16)

    def gate(ref):
        return 1.0 / (1.0 + jnp.exp2(ref[0].astype(F32)))

    merged = (gate(ga_ref) * _dot(ya_ref[0], wb_ref[0])
              + gate(gb_ref) * _dot(yb_ref[0], wb_ref[1])
              + gate(gc_ref) * _dot(y_c, wb_ref[2]))
    y = _dot(merged.astype(BF16), wo_ref[...])

    r = alpha * x_ref[0] + (1.0 + gate_ref[0]) * y
    x_new = _standardize(r) * lng_ref[...] + lnb_ref[...]
    xo_ref[0] = x_new
    if emit_h:
        d = x_new.shape[-1]
        shift = nmod_ref[0, :, 0:d]
        scale = nmod_ref[0, :, d:2 * d]
        h_ref[0] = (_standardize(x_new) * (1.0 + scale) + shift).astype(h_ref.dtype)


def _merge(ya, yb, proj, x, mod_l, conv_w_l, wb_l, wo_l, lng_l, lnb_l, alpha, next_mod):
    bsz, s, d = x.shape
    tm = MERGE_TM
    cw = CONV_WIDTH
    emit_h = next_mod is not None
    halo_blocks = tm // HALO_ROWS

    def halo_map(col):
        return lambda b, i: (b, jnp.maximum(i * halo_blocks - 1, 0), col)

    in_specs = [
        pl.BlockSpec((1, tm, SB_WIDTH), lambda b, i: (b, i, 0)),
        pl.BlockSpec((1, tm, HG_WIDTH), lambda b, i: (b, i, 0)),
        pl.BlockSpec((1, tm, cw), lambda b, i: (b, i, COL_PRE)),
        pl.BlockSpec((1, tm, cw), lambda b, i: (b, i, COL_POST)),
        pl.BlockSpec((1, tm, cw), lambda b, i: (b, i, COL_U)),
        pl.BlockSpec((1, tm, cw), lambda b, i: (b, i, COL_ZC)),
        pl.BlockSpec((1, HALO_ROWS, cw), halo_map(COL_PRE)),
        pl.BlockSpec((1, HALO_ROWS, cw), halo_map(COL_U)),
        pl.BlockSpec((1, tm, d), lambda b, i: (b, i, COL_GATES)),
        pl.BlockSpec((1, tm, d), lambda b, i: (b, i, COL_GATES + 1)),
        pl.BlockSpec((1, tm, d), lambda b, i: (b, i, COL_GATES + 2)),
        pl.BlockSpec((1, tm, d), lambda b, i: (b, i, 0)),
        pl.BlockSpec((1, 1, d), lambda b, i: (b, 0, 2)),
        pl.BlockSpec((CONV_K, cw), lambda b, i: (0, 0)),
        pl.BlockSpec((3, SB_WIDTH, d), lambda b, i: (0, 0, 0)),
        pl.BlockSpec((d, d), lambda b, i: (0, 0)),
        pl.BlockSpec((1, d), lambda b, i: (0, 0)),
        pl.BlockSpec((1, d), lambda b, i: (0, 0)),
    ]
    args = [ya, yb] + [proj] * 9 + [x, mod_l, conv_w_l, wb_l, wo_l,
            lng_l.reshape(1, d), lnb_l.reshape(1, d)]
    out_shape = [jax.ShapeDtypeStruct((bsz, s, d), F32)]
    out_specs = [pl.BlockSpec((1, tm, d), lambda b, i: (b, i, 0))]
    if emit_h:
        in_specs.append(pl.BlockSpec((1, 1, 3 * d), lambda b, i: (b, 0, 0)))
        args.append(next_mod)
        out_shape.append(jax.ShapeDtypeStruct((bsz, s, d), BF16))
        out_specs.append(pl.BlockSpec((1, tm, d), lambda b, i: (b, i, 0)))
    outs = pl.pallas_call(
        partial(_merge_kernel, alpha, emit_h),
        out_shape=out_shape,
        grid=(bsz, s // tm),
        in_specs=in_specs,
        out_specs=out_specs,
        scratch_shapes=[pltpu.VMEM((tm + SUBLANES, cw), F32)],
        compiler_params=pltpu.CompilerParams(
            dimension_semantics=("arbitrary", "arbitrary"),
            vmem_limit_bytes=V7X_VMEM_LIMIT_BYTES),
        name="merge_residual",
    )(*args)
    return (outs[0], outs[1]) if emit_h else (outs[0], None)


def _in_proj_col_scale():
    cs = np.ones((1, IN_COLS), np.float32)
    cs[:, COL_QA * 512:(COL_QA + 1) * 512] = LOG2E * SB_HEAD_DIM ** -0.5
    cs[:, COL_GATES * D_MODEL:] = -LOG2E
    return jnp.asarray(cs)


def kernel(x, c, w_mod, b_mod, w_in, conv_w, hgrn_norm_w, lower_bounds, w_branch, w_out, ln_g, ln_b):
    bsz, s, d = x.shape
    depth = w_mod.shape[0]
    alpha = (2.0 * depth) ** 0.25

    mod = _modulation(c, w_mod, b_mod).reshape(depth, bsz, 1, 3 * d)
    col_scale = _in_proj_col_scale()

    h = _lnmod(x, mod[0])
    for l in range(depth):
        proj = _in_proj(h.reshape(bsz * s, d), w_in, l, col_scale).reshape(bsz, s, IN_COLS)
        ya = _stick_breaking(proj)
        yb = _hgrn2(proj, lower_bounds, hgrn_norm_w[l], l)
        next_mod = mod[l + 1] if l + 1 < depth else None
        x, h = _merge(ya, yb, proj, x, mod[l], conv_w[l], w_branch[l].astype(BF16),
                      w_out[l].astype(BF16), ln_g[l], ln_b[l], alpha, next_mod)
    return x
```

```python
from functools import partial

import numpy as np
import jax
import jax.numpy as jnp
from jax import lax
from jax.experimental import pallas as pl
from jax.experimental.pallas import tpu as pltpu

D_MODEL = 1024
SB_HEADS = 8
SB_HEAD_DIM = 64
SB_WIDTH = SB_HEADS * SB_HEAD_DIM
HG_HEADS = 4
HG_HEAD_DIM = 128
HG_WIDTH = HG_HEADS * HG_HEAD_DIM
CONV_WIDTH = 512
CONV_K = 3
LN_EPS = 1e-5
RMS_EPS = 1e-6

F32 = jnp.float32
BF16 = jnp.bfloat16

V7X_VMEM_LIMIT_BYTES = 56 * 1024 * 1024
SUBLANES = 8
HALO_ROWS = 16

IN_COLS = 12 * 512 + 3 * D_MODEL
(COL_QA, COL_KA, COL_VA, COL_ZA, COL_QB, COL_FB, COL_IB, COL_ZB,
 COL_PRE, COL_POST, COL_U, COL_ZC) = range(12)
COL_GATES = 12 * 512 // D_MODEL
IN_PROJ_TM = 4096
IN_PROJ_TN = 768
MERGE_TM = 512
LNMOD_TM = 1024

SB_GROUP = 4
SB_GROUP_W = SB_GROUP * SB_HEAD_DIM
SB_N_GROUPS = SB_HEADS // SB_GROUP
SB_TQ = 512
SB_TK = 256
SB_N_DIAG = SB_TQ // SB_TK
SB_PART_ROWS = SB_GROUP * SB_TK
SB_BPS = 2
SB_CHAINS = [(bb, g) for bb in range(SB_BPS) for g in range(SB_N_GROUPS)]
LOG2E = 1.4426950408889634
LOG2E_BF16_HI = 1.4453125
LOG2E_BF16_LO = -0.00262451171875

HG_C = 128
HG_CPS = 4
HG_LEVELS = 7
HG_BAND = 3
HG_MM_LEVELS = range(3, HG_LEVELS + 1)
HG_NMAT = len(HG_MM_LEVELS) + 2


def _nt_dot(a, b):
    return lax.dot_general(a, b, (((1,), (1,)), ((), ())), preferred_element_type=F32)


def _dot(a, b):
    return jnp.dot(a, b, preferred_element_type=F32)


def _split_bf16(v):
    hi = v.astype(BF16)
    lo = (v - hi.astype(F32)).astype(BF16)
    return hi, lo


def _sigmoid(v):
    return 1.0 / (1.0 + jnp.exp(-v))


def _silu(v):
    return v * _sigmoid(v)


def _standardize(xf):
    mu = jnp.mean(xf, axis=-1, keepdims=True)
    xc = xf - mu
    var = jnp.mean(xc * xc, axis=-1, keepdims=True)
    return xc * lax.rsqrt(var + LN_EPS)


def _mod_kernel(c_ref, w_ref, b_ref, o_ref):
    c_hi, c_lo = _split_bf16(c_ref[...])
    w_hi, w_lo = _split_bf16(w_ref[0])
    acc = _dot(c_hi, w_hi) + (_dot(c_hi, w_lo) + _dot(c_lo, w_hi))
    o_ref[0] = acc + b_ref[0]


def _modulation(c, w_mod, b_mod):
    depth, d, n = w_mod.shape
    bsz = c.shape[0]
    tn = 1024
    return pl.pallas_call(
        _mod_kernel,
        out_shape=jax.ShapeDtypeStruct((depth, bsz, n), F32),
        grid=(depth, n // tn),
        in_specs=[
            pl.BlockSpec((bsz, d), lambda l, j: (0, 0)),
            pl.BlockSpec((1, d, tn), lambda l, j: (l, 0, j)),
            pl.BlockSpec((1, 1, tn), lambda l, j: (l, 0, j)),
        ],
        out_specs=pl.BlockSpec((1, bsz, tn), lambda l, j: (l, 0, j)),
        compiler_params=pltpu.CompilerParams(
            dimension_semantics=("arbitrary", "arbitrary"),
            vmem_limit_bytes=V7X_VMEM_LIMIT_BYTES),
        name="adaln_mod",
    )(c, w_mod, b_mod.reshape(depth, 1, n))


def _lnmod_kernel(x_ref, mod_ref, h_ref):
    d = x_ref.shape[-1]
    shift = mod_ref[0, :, 0:d]
    scale = mod_ref[0, :, d:2 * d]
    h_ref[0] = (_standardize(x_ref[0]) * (1.0 + scale) + shift).astype(h_ref.dtype)


def _lnmod(x, mod_l):
    bsz, s, d = x.shape
    tm = LNMOD_TM
    return pl.pallas_call(
        _lnmod_kernel,
        out_shape=jax.ShapeDtypeStruct((bsz, s, d), BF16),
        grid=(bsz, s // tm),
        in_specs=[
            pl.BlockSpec((1, tm, d), lambda b, i: (b, i, 0)),
            pl.BlockSpec((1, 1, 3 * d), lambda b, i: (b, 0, 0)),
        ],
        out_specs=pl.BlockSpec((1, tm, d), lambda b, i: (b, i, 0)),
        compiler_params=pltpu.CompilerParams(
            dimension_semantics=("arbitrary", "arbitrary"),
            vmem_limit_bytes=V7X_VMEM_LIMIT_BYTES),
        name="ln_modulate",
    )(x, mod_l)


def _in_proj_kernel(h_ref, w_ref, cs_ref, o_ref, wbf_ref):
    @pl.when(pl.program_id(1) == 0)
    def _():
        wbf_ref[...] = (w_ref[0] * cs_ref[...]).astype(BF16)

    o_ref[...] = _dot(h_ref[...], wbf_ref[...]).astype(o_ref.dtype)


def _in_proj(h2, w_in, layer, col_scale):
    m, k = h2.shape
    n = w_in.shape[2]
    tm, tn = IN_PROJ_TM, IN_PROJ_TN
    return pl.pallas_call(
        _in_proj_kernel,
        out_shape=jax.ShapeDtypeStruct((m, n), BF16),
        grid=(n // tn, m // tm),
        in_specs=[
            pl.BlockSpec((tm, k), lambda j, i: (i, 0)),
            pl.BlockSpec((1, k, tn), lambda j, i: (layer, 0, j)),
            pl.BlockSpec((1, tn), lambda j, i: (0, j)),
        ],
        out_specs=pl.BlockSpec((tm, tn), lambda j, i: (i, j)),
        scratch_shapes=[pltpu.VMEM((k, tn), BF16)],
        compiler_params=pltpu.CompilerParams(
            dimension_semantics=("arbitrary", "arbitrary"),
            vmem_limit_bytes=V7X_VMEM_LIMIT_BYTES),
        name="in_proj",
    )(h2, w_in, col_scale)


def _sb_band_mask():
    t_in = np.arange(SB_PART_ROWS)[:, None] % SB_TK
    tri = np.arange(SB_TK)[None, :] < t_in
    ones = np.ones(((SB_N_DIAG - 1) * SB_PART_ROWS, SB_TK), bool)
    return np.concatenate([tri, ones], axis=0).astype(np.float32)


def _sb_kernel(q_ref, k_ref, v_ref, z_ref, dm_ref, o_ref, qs_ref, acc_ref, car_ref):
    qi = pl.program_id(1)
    tk, gw = SB_TK, SB_GROUP_W
    rows = SB_N_DIAG * SB_PART_ROWS

    lane_head = lax.broadcasted_iota(jnp.int32, (tk, gw), 1) // SB_HEAD_DIM
    head_row = lax.broadcasted_iota(jnp.int32, (1, gw), 1) // SB_HEAD_DIM
    head_keep = [jnp.where(head_row == h, 1.0, 0.0).astype(BF16) for h in range(SB_GROUP)]
    for c, (bb, g) in enumerate(SB_CHAINS):
        for p in range(SB_N_DIAG):
            q_bf = q_ref[bb, p * tk:(p + 1) * tk, g * gw:(g + 1) * gw]
            qs_ref[c, p * SB_PART_ROWS:(p + 1) * SB_PART_ROWS, :] = jnp.concatenate(
                [q_bf * head_keep[h] for h in range(SB_GROUP)], axis=0)

    jj = lax.broadcasted_iota(jnp.int32, (tk, tk), 0)
    ss = lax.broadcasted_iota(jnp.int32, (tk, tk), 1)
    u_mat = jnp.where(jj >= ss, 1.0, 0.0).astype(BF16)

    car_ref[...] = jnp.zeros_like(car_ref)

    def softplus2_bf16(z):
        zb = z.astype(BF16)
        ln_w = jnp.log(1.0 + jnp.exp2(-jnp.abs(zb)))
        return jnp.maximum(zb, 0.0) + (ln_w * LOG2E_BF16_HI + ln_w * LOG2E_BF16_LO)

    def block(kb, band):
        start = pl.multiple_of(kb * tk, tk)
        r0 = 0 if band is None else band * SB_PART_ROWS
        rws = slice(r0, rows)
        for c, (bb, g) in enumerate(SB_CHAINS):
            gcols = slice(g * gw, (g + 1) * gw)
            z = _nt_dot(qs_ref[c, rws, :], k_ref[bb, pl.ds(start, tk), gcols])
            sp = softplus2_bf16(z)
            if band is not None:
                sp = sp * dm_ref[0:rows - r0, :]
            rb = _dot(sp, u_mat)
            arg = z - (rb + car_ref[c, rws, :])
            if band is not None:
                a = jnp.exp2(jnp.minimum(arg, 0.0).astype(BF16)) * dm_ref[0:rows - r0, :]
            else:
                a = jnp.exp2(arg.astype(BF16))
            av = _dot(a, v_ref[bb, pl.ds(start, tk), gcols])
            if band is None:
                acc_ref[c, rws, :] += av
            else:
                r1 = r0 + SB_PART_ROWS
                acc_ref[c, r0:r1, :] = av[0:SB_PART_ROWS]
                if r1 < rows:
                    acc_ref[c, r1:rows, :] += av[SB_PART_ROWS:]
            car_ref[c, rws, :] += rb[:, 0:1]

    first_kb = qi * SB_N_DIAG
    for d in reversed(range(SB_N_DIAG)):
        block(first_kb + d, d)

    def body(i, carry):
        block(first_kb - 1 - i, None)
        return carry

    lax.fori_loop(0, first_kb, body, 0)

    for c, (bb, g) in enumerate(SB_CHAINS):
        for p in range(SB_N_DIAG):
            r = p * SB_PART_ROWS
            o = acc_ref[c, r:r + tk, :]
            for h in range(1, SB_GROUP):
                o = jnp.where(lane_head == h, acc_ref[c, r + h * tk:r + (h + 1) * tk, :], o)
            zg = z_ref[bb, p * tk:(p + 1) * tk, g * gw:(g + 1) * gw].astype(F32)
            o_ref[bb, p * tk:(p + 1) * tk, g * gw:(g + 1) * gw] = (
                o * _silu(zg)).astype(o_ref.dtype)


def _stick_breaking(proj):
    bsz, s, _ = proj.shape
    w = SB_WIDTH
    n_chains = len(SB_CHAINS)
    rows = SB_N_DIAG * SB_PART_ROWS
    return pl.pallas_call(
        _sb_kernel,
        out_shape=jax.ShapeDtypeStruct((bsz, s, w), BF16),
        grid=(bsz // SB_BPS, s // SB_TQ),
        in_specs=[
            pl.BlockSpec((SB_BPS, SB_TQ, w), lambda b, i: (b, i, COL_QA)),
            pl.BlockSpec((SB_BPS, s, w), lambda b, i: (b, 0, COL_KA)),
            pl.BlockSpec((SB_BPS, s, w), lambda b, i: (b, 0, COL_VA)),
            pl.BlockSpec((SB_BPS, SB_TQ, w), lambda b, i: (b, i, COL_ZA)),
            pl.BlockSpec((rows, SB_TK), lambda b, i: (0, 0)),
        ],
        out_specs=pl.BlockSpec((SB_BPS, SB_TQ, w), lambda b, i: (b, i, 0)),
        scratch_shapes=[
            pltpu.VMEM((n_chains, rows, SB_GROUP_W), BF16),
            pltpu.VMEM((n_chains, rows, SB_GROUP_W), F32),
            pltpu.VMEM((n_chains, rows, 1), F32),
        ],
        compiler_params=pltpu.CompilerParams(
            dimension_semantics=("arbitrary", "arbitrary"),
            vmem_limit_bytes=V7X_VMEM_LIMIT_BYTES),
        name="stick_breaking",
    )(proj, proj, proj, proj, jnp.asarray(_sb_band_mask(), BF16))


def _hgrn_sum_matrix():
    c = HG_C
    w = np.zeros((HG_NMAT, c, c), np.float32)
    for i, lvl in enumerate(HG_MM_LEVELS):
        n = 1 << lvl
        for r in range(c):
            m = (r // n) * n + n // 2
            if r >= m:
                w[i, r, m:r + 1] = 1.0
            else:
                w[i, r, r + 1:m] = 1.0
    for r in range(c):
        w[HG_NMAT - 2, r, :r + 1] = 1.0
        w[HG_NMAT - 1, r, r + 1:] = 1.0
    return w.reshape(HG_NMAT * c, c)


def _hgrn_level_matrix():
    c = HG_C
    lv = np.full((c, c), -1, np.int32)
    for t in range(c):
        for s in range(t + 1):
            lv[t, s] = t - s if t - s <= HG_BAND else HG_BAND + (t ^ s).bit_length()
    return lv


def _hgrn_kernel(layer, q_ref, f_ref, i_ref, z_ref, lb_ref, nw_ref, w_ref, lv_ref, o_ref,
                 state_ref):
    c = HG_C
    dk = HG_HEAD_DIM
    ci = pl.program_id(1)

    @pl.when(ci == 0)
    def _():
        state_ref[...] = jnp.zeros_like(state_ref)

    lbw = lb_ref[...]
    e = jnp.exp(lbw - jnp.max(lbw, axis=0, keepdims=True))
    p = e / jnp.sum(e, axis=0, keepdims=True)
    lb_all = jnp.zeros((1, HG_WIDTH), F32)
    for l in range(1, layer + 1):
        lb_all = lb_all + p[l:l + 1]

    lv = lv_ref[...]
    for cc, h in [(cc, h) for cc in range(HG_CPS) for h in range(HG_HEADS)]:
        rws = slice(cc * c, (cc + 1) * c)
        cols = slice(h * dk, (h + 1) * dk)
        if h == 0:
            f_all = lb_all + (1.0 - lb_all) * _sigmoid(f_ref[0, rws, :].astype(F32))
            g_all = jnp.log(f_all) * LOG2E
            g_hi, g_lo = _split_bf16(g_all)
        if h % 2 == 0:
            pair = slice(h * dk, (h + 2) * dk)
            g_cat = jnp.concatenate([g_hi[:, pair], g_lo[:, pair]], axis=0)
            ex_pair = jnp.exp2(_dot(w_ref[...], g_cat))
            ex_pair_bf = ex_pair.astype(BF16)
        half = slice((h % 2) * dk, (h % 2 + 1) * dk)
        ex_bf = ex_pair_bf[:, half]
        f_h = f_all[:, cols]
        k_f = 1.0 - f_h
        q_f = _silu(q_ref[0, rws, cols].astype(F32))
        k_bf = k_f.astype(BF16)
        q_bf = q_f.astype(BF16)
        v_bf = i_ref[0, rws, cols]

        scores = jnp.where(lv == 0, jnp.sum(q_f * k_f, axis=-1, keepdims=True), 0.0)
        qd = q_f
        for j in range(1, HG_BAND + 1):
            qd = qd * (f_h if j == 1 else pltpu.roll(f_h, j - 1, axis=0))
            d_j = jnp.sum(qd * pltpu.roll(k_f, j, axis=0), axis=-1, keepdims=True)
            scores = jnp.where(lv == j, d_j, scores)
        for i, lvl in enumerate(HG_MM_LEVELS):
            dec = ex_bf[i * c:(i + 1) * c]
            scores = jnp.where(lv == HG_BAND + lvl, _nt_dot(q_bf * dec, k_bf * dec), scores)

        dec_q = ex_bf[(HG_NMAT - 2) * c:(HG_NMAT - 1) * c]
        dec_k = ex_bf[(HG_NMAT - 1) * c:HG_NMAT * c]
        state = state_ref[h]
        inter = _nt_dot(q_bf * dec_q, state.astype(BF16))
        intra = _dot(scores.astype(BF16), v_bf)
        o = inter + intra

        v_t = v_bf.astype(F32).T.astype(BF16)
        dec_end = ex_pair[(HG_NMAT - 1) * c - 1:(HG_NMAT - 1) * c, half]
        state_ref[h] = state * dec_end + _dot(v_t, k_bf * dec_k)

        ms = jnp.mean(o * o, axis=-1, keepdims=True)
        o = o * lax.rsqrt(ms + RMS_EPS) * nw_ref[...]
        o_ref[0, rws, cols] = (o * _silu(z_ref[0, rws, cols].astype(F32))).astype(o_ref.dtype)


def _hgrn2(proj, lower_bounds, norm_w_l, layer):
    bsz, s, _ = proj.shape
    depth = lower_bounds.shape[0]
    dk = HG_HEAD_DIM
    w = HG_WIDTH
    w_one = _hgrn_sum_matrix()
    w_sum = jnp.asarray(np.concatenate([w_one, w_one], axis=1), BF16)
    lv = jnp.asarray(_hgrn_level_matrix())
    tr = HG_CPS * HG_C
    return pl.pallas_call(
        partial(_hgrn_kernel, layer),
        out_shape=jax.ShapeDtypeStruct((bsz, s, w), BF16),
        grid=(bsz, s // tr),
        in_specs=[
            pl.BlockSpec((1, tr, w), lambda b, i: (b, i, COL_QB)),
            pl.BlockSpec((1, tr, w), lambda b, i: (b, i, COL_FB)),
            pl.BlockSpec((1, tr, w), lambda b, i: (b, i, COL_IB)),
            pl.BlockSpec((1, tr, w), lambda b, i: (b, i, COL_ZB)),
            pl.BlockSpec((depth, w), lambda b, i: (0, 0)),
            pl.BlockSpec((1, dk), lambda b, i: (0, 0)),
            pl.BlockSpec((HG_NMAT * HG_C, 2 * HG_C), lambda b, i: (0, 0)),
            pl.BlockSpec((HG_C, HG_C), lambda b, i: (0, 0)),
        ],
        out_specs=pl.BlockSpec((1, tr, w), lambda b, i: (b, i, 0)),
        scratch_shapes=[pltpu.VMEM((HG_HEADS, dk, dk), F32)],
        compiler_params=pltpu.CompilerParams(
            dimension_semantics=("arbitrary", "arbitrary"),
            vmem_limit_bytes=V7X_VMEM_LIMIT_BYTES),
        name="hgrn2",
    )(proj, proj, proj, proj, lower_bounds, norm_w_l.reshape(1, dk), w_sum, lv)


def _merge_kernel(alpha, emit_h, ya_ref, yb_ref, pre_ref, post_ref, u_ref, zc_ref,
                  pre_h_ref, u_h_ref, ga_ref, gb_ref, gc_ref, x_ref, gate_ref, cw_ref,
                  wb_ref, wo_ref, lng_ref, lnb_ref, *rest):
    if emit_h:
        nmod_ref, xo_ref, h_ref, ext_ref = rest
    else:
        xo_ref, ext_ref = rest
    i = pl.program_id(1)
    tm = pre_ref.shape[1]

    hp = pre_h_ref[0, HALO_ROWS - SUBLANES:HALO_ROWS, :].astype(F32)
    hu = u_h_ref[0, HALO_ROWS - SUBLANES:HALO_ROWS, :].astype(F32)
    ext_ref[0:SUBLANES, :] = jnp.where(i > 0, hp * hu, 0.0)
    ext_ref[SUBLANES:SUBLANES + tm, :] = pre_ref[0].astype(F32) * u_ref[0].astype(F32)
    cw = cw_ref[...]
    conv = (cw[2:3] * ext_ref[SUBLANES:SUBLANES + tm, :]
            + cw[1:2] * ext_ref[SUBLANES - 1:SUBLANES - 1 + tm, :]
            + cw[0:1] * ext_ref[SUBLANES - 2:SUBLANES - 2 + tm, :])
    y_c = (post_ref[0].astype(F32) * conv * _silu(zc_ref[0].astype(F32))).astype(BF16)

    def gate(ref):
        return 1.0 / (1.0 + jnp.exp2(ref[0].astype(F32)))

    merged = (gate(ga_ref) * _dot(ya_ref[0], wb_ref[0])
              + gate(gb_ref) * _dot(yb_ref[0], wb_ref[1])
              + gate(gc_ref) * _dot(y_c, wb_ref[2]))
    y = _dot(merged.astype(BF16), wo_ref[...])

    r = alpha * x_ref[0] + (1.0 + gate_ref[0]) * y
    x_new = _standardize(r) * lng_ref[...] + lnb_ref[...]
    xo_ref[0] = x_new
    if emit_h:
        d = x_new.shape[-1]
        shift = nmod_ref[0, :, 0:d]
        scale = nmod_ref[0, :, d:2 * d]
        h_ref[0] = (_standardize(x_new) * (1.0 + scale) + shift).astype(h_ref.dtype)


def _merge(ya, yb, proj, x, mod_l, conv_w_l, wb_l, wo_l, lng_l, lnb_l, alpha, next_mod):
    bsz, s, d = x.shape
    tm = MERGE_TM
    cw = CONV_WIDTH
    emit_h = next_mod is not None
    halo_blocks = tm // HALO_ROWS

    def halo_map(col):
        return lambda b, i: (b, jnp.maximum(i * halo_blocks - 1, 0), col)

    in_specs = [
        pl.BlockSpec((1, tm, SB_WIDTH), lambda b, i: (b, i, 0)),
        pl.BlockSpec((1, tm, HG_WIDTH), lambda b, i: (b, i, 0)),
        pl.BlockSpec((1, tm, cw), lambda b, i: (b, i, COL_PRE)),
        pl.BlockSpec((1, tm, cw), lambda b, i: (b, i, COL_POST)),
        pl.BlockSpec((1, tm, cw), lambda b, i: (b, i, COL_U)),
        pl.BlockSpec((1, tm, cw), lambda b, i: (b, i, COL_ZC)),
        pl.BlockSpec((1, HALO_ROWS, cw), halo_map(COL_PRE)),
        pl.BlockSpec((1, HALO_ROWS, cw), halo_map(COL_U)),
        pl.BlockSpec((1, tm, d), lambda b, i: (b, i, COL_GATES)),
        pl.BlockSpec((1, tm, d), lambda b, i: (b, i, COL_GATES + 1)),
        pl.BlockSpec((1, tm, d), lambda b, i: (b, i, COL_GATES + 2)),
        pl.BlockSpec((1, tm, d), lambda b, i: (b, i, 0)),
        pl.BlockSpec((1, 1, d), lambda b, i: (b, 0, 2)),
        pl.BlockSpec((CONV_K, cw), lambda b, i: (0, 0)),
        pl.BlockSpec((3, SB_WIDTH, d), lambda b, i: (0, 0, 0)),
        pl.BlockSpec((d, d), lambda b, i: (0, 0)),
        pl.BlockSpec((1, d), lambda b, i: (0, 0)),
        pl.BlockSpec((1, d), lambda b, i: (0, 0)),
    ]
    args = [ya, yb] + [proj] * 9 + [x, mod_l, conv_w_l, wb_l, wo_l,
            lng_l.reshape(1, d), lnb_l.reshape(1, d)]
    out_shape = [jax.ShapeDtypeStruct((bsz, s, d), F32)]
    out_specs = [pl.BlockSpec((1, tm, d), lambda b, i: (b, i, 0))]
    if emit_h:
        in_specs.append(pl.BlockSpec((1, 1, 3 * d), lambda b, i: (b, 0, 0)))
        args.append(next_mod)
        out_shape.append(jax.ShapeDtypeStruct((bsz, s, d), BF16))
        out_specs.append(pl.BlockSpec((1, tm, d), lambda b, i: (b, i, 0)))
    outs = pl.pallas_call(
        partial(_merge_kernel, alpha, emit_h),
        out_shape=out_shape,
        grid=(bsz, s // tm),
        in_specs=in_specs,
        out_specs=out_specs,
        scratch_shapes=[pltpu.VMEM((tm + SUBLANES, cw), F32)],
        compiler_params=pltpu.CompilerParams(
            dimension_semantics=("arbitrary", "arbitrary"),
            vmem_limit_bytes=V7X_VMEM_LIMIT_BYTES),
        name="merge_residual",
    )(*args)
    return (outs[0], outs[1]) if emit_h else (outs[0], None)


def _in_proj_col_scale():
    cs = np.ones((1, IN_COLS), np.float32)
    cs[:, COL_QA * 512:(COL_QA + 1) * 512] = LOG2E * SB_HEAD_DIM ** -0.5
    cs[:, COL_GATES * D_MODEL:] = -LOG2E
    return jnp.asarray(cs)


def kernel(x, c, w_mod, b_mod, w_in, conv_w, hgrn_norm_w, lower_bounds, w_branch, w_out, ln_g, ln_b):
    bsz, s, d = x.shape
    depth = w_mod.shape[0]
    alpha = (2.0 * depth) ** 0.25

    mod = _modulation(c, w_mod, b_mod).reshape(depth, bsz, 1, 3 * d)
    col_scale = _in_proj_col_scale()

    h = _lnmod(x, mod[0])
    for l in range(depth):
        proj = _in_proj(h.reshape(bsz * s, d), w_in, l, col_scale).reshape(bsz, s, IN_COLS)
        ya = _stick_breaking(proj)
        yb = _hgrn2(proj, lower_bounds, hgrn_norm_w[l], l)
        next_mod = mod[l + 1] if l + 1 < depth else None
        x, h = _merge(ya, yb, proj, x, mod[l], conv_w[l], w_branch[l].astype(BF16),
                      w_out[l].astype(BF16), ln_g[l], ln_b[l], alpha, next_mod)
    return x
```

```python
from functools import partial

import numpy as np
import jax
import jax.numpy as jnp
from jax import lax
from jax.experimental import pallas as pl
from jax.experimental.pallas import tpu as pltpu

D_MODEL = 1024
SB_HEADS = 8
SB_HEAD_DIM = 64
SB_WIDTH = SB_HEADS * SB_HEAD_DIM
HG_HEADS = 4
HG_HEAD_DIM = 128
HG_WIDTH = HG_HEADS * HG_HEAD_DIM
CONV_WIDTH = 512
CONV_K = 3
LN_EPS = 1e-5
RMS_EPS = 1e-6

F32 = jnp.float32
BF16 = jnp.bfloat16

V7X_VMEM_LIMIT_BYTES = 56 * 1024 * 1024
SUBLANES = 8
HALO_ROWS = 16

IN_COLS = 12 * 512 + 3 * D_MODEL
(COL_QA, COL_KA, COL_VA, COL_ZA, COL_QB, COL_FB, COL_IB, COL_ZB,
 COL_PRE, COL_POST, COL_U, COL_ZC) = range(12)
COL_GATES = 12 * 512 // D_MODEL
IN_PROJ_TM = 2048
IN_PROJ_TN = 1536
MERGE_TM = 512
LNMOD_TM = 1024

SB_GROUP = 4
SB_GROUP_W = SB_GROUP * SB_HEAD_DIM
SB_N_GROUPS = SB_HEADS // SB_GROUP
SB_TQ = 512
SB_TK = 256
SB_N_DIAG = SB_TQ // SB_TK
SB_PART_ROWS = SB_GROUP * SB_TK
SB_BPS = 2
SB_UNDERFLOW_BITS = 160.0
SB_BOUND_SLACK = 1.05
SB_CHAINS = [(bb, g) for bb in range(SB_BPS) for g in range(SB_N_GROUPS)]
LOG2E = 1.4426950408889634
LOG2E_BF16_HI = 1.4453125
LOG2E_BF16_LO = -0.00262451171875

HG_C = 128
HG_CPS = 4
HG_LEVELS = 7
HG_BAND = 3
HG_MM_LEVELS = range(3, HG_LEVELS + 1)
HG_NMAT = len(HG_MM_LEVELS) + 2


def _nt_dot(a, b):
    return lax.dot_general(a, b, (((1,), (1,)), ((), ())), preferred_element_type=F32)


def _dot(a, b):
    return jnp.dot(a, b, preferred_element_type=F32)


def _split_bf16(v):
    hi = v.astype(BF16)
    lo = (v - hi.astype(F32)).astype(BF16)
    return hi, lo


def _sigmoid(v):
    return 1.0 / (1.0 + jnp.exp(-v))


def _silu(v):
    return v * _sigmoid(v)


def _standardize(xf):
    mu = jnp.mean(xf, axis=-1, keepdims=True)
    xc = xf - mu
    var = jnp.mean(xc * xc, axis=-1, keepdims=True)
    return xc * lax.rsqrt(var + LN_EPS)


def _mod_kernel(c_ref, w_ref, b_ref, o_ref):
    c_hi, c_lo = _split_bf16(c_ref[...])
    w_hi, w_lo = _split_bf16(w_ref[0])
    acc = _dot(c_hi, w_hi) + (_dot(c_hi, w_lo) + _dot(c_lo, w_hi))
    o_ref[0] = acc + b_ref[0]


def _modulation(c, w_mod, b_mod):
    depth, d, n = w_mod.shape
    bsz = c.shape[0]
    tn = 1024
    return pl.pallas_call(
        _mod_kernel,
        out_shape=jax.ShapeDtypeStruct((depth, bsz, n), F32),
        grid=(depth, n // tn),
        in_specs=[
            pl.BlockSpec((bsz, d), lambda l, j: (0, 0)),
            pl.BlockSpec((1, d, tn), lambda l, j: (l, 0, j)),
            pl.BlockSpec((1, 1, tn), lambda l, j: (l, 0, j)),
        ],
        out_specs=pl.BlockSpec((1, bsz, tn), lambda l, j: (l, 0, j)),
        compiler_params=pltpu.CompilerParams(
            dimension_semantics=("arbitrary", "arbitrary"),
            vmem_limit_bytes=V7X_VMEM_LIMIT_BYTES),
        name="adaln_mod",
    )(c, w_mod, b_mod.reshape(depth, 1, n))


def _lnmod_kernel(x_ref, mod_ref, h_ref):
    d = x_ref.shape[-1]
    shift = mod_ref[0, :, 0:d]
    scale = mod_ref[0, :, d:2 * d]
    h_ref[0] = (_standardize(x_ref[0]) * (1.0 + scale) + shift).astype(h_ref.dtype)


def _lnmod(x, mod_l):
    bsz, s, d = x.shape
    tm = LNMOD_TM
    return pl.pallas_call(
        _lnmod_kernel,
        out_shape=jax.ShapeDtypeStruct((bsz, s, d), BF16),
        grid=(bsz, s // tm),
        in_specs=[
            pl.BlockSpec((1, tm, d), lambda b, i: (b, i, 0)),
            pl.BlockSpec((1, 1, 3 * d), lambda b, i: (b, 0, 0)),
        ],
        out_specs=pl.BlockSpec((1, tm, d), lambda b, i: (b, i, 0)),
        compiler_params=pltpu.CompilerParams(
            dimension_semantics=("arbitrary", "arbitrary"),
            vmem_limit_bytes=V7X_VMEM_LIMIT_BYTES),
        name="ln_modulate",
    )(x, mod_l)


def _in_proj_kernel(h_ref, w_ref, cs_ref, o_ref, wbf_ref):
    @pl.when(pl.program_id(1) == 0)
    def _():
        wbf_ref[...] = (w_ref[0] * cs_ref[...]).astype(BF16)

    o_ref[...] = _dot(h_ref[...], wbf_ref[...]).astype(o_ref.dtype)


def _in_proj(h2, w_in, layer, col_scale):
    m, k = h2.shape
    n = w_in.shape[2]
    tm, tn = IN_PROJ_TM, IN_PROJ_TN
    return pl.pallas_call(
        _in_proj_kernel,
        out_shape=jax.ShapeDtypeStruct((m, n), BF16),
        grid=(n // tn, m // tm),
        in_specs=[
            pl.BlockSpec((tm, k), lambda j, i: (i, 0)),
            pl.BlockSpec((1, k, tn), lambda j, i: (layer, 0, j)),
            pl.BlockSpec((1, tn), lambda j, i: (0, j)),
        ],
        out_specs=pl.BlockSpec((tm, tn), lambda j, i: (i, j)),
        scratch_shapes=[pltpu.VMEM((k, tn), BF16)],
        compiler_params=pltpu.CompilerParams(
            dimension_semantics=("arbitrary", "arbitrary"),
            vmem_limit_bytes=V7X_VMEM_LIMIT_BYTES),
        name="in_proj",
    )(h2, w_in, col_scale)


def _sb_band_mask():
    t_in = np.arange(SB_PART_ROWS)[:, None] % SB_TK
    tri = np.arange(SB_TK)[None, :] < t_in
    ones = np.ones(((SB_N_DIAG - 1) * SB_PART_ROWS, SB_TK), bool)
    return np.concatenate([tri, ones], axis=0).astype(np.float32)


def _sb_kernel(q_ref, k_ref, v_ref, z_ref, dm_ref, o_ref, qs_ref, acc_ref, car_ref, kmax2_ref):
    qi = pl.program_id(1)
    tk, gw = SB_TK, SB_GROUP_W
    rows = SB_N_DIAG * SB_PART_ROWS

    lane_head = lax.broadcasted_iota(jnp.int32, (tk, gw), 1) // SB_HEAD_DIM
    head_row = lax.broadcasted_iota(jnp.int32, (1, gw), 1) // SB_HEAD_DIM
    head_keep = [jnp.where(head_row == h, 1.0, 0.0).astype(BF16) for h in range(SB_GROUP)]
    for c, (bb, g) in enumerate(SB_CHAINS):
        for p in range(SB_N_DIAG):
            q_bf = q_ref[bb, p * tk:(p + 1) * tk, g * gw:(g + 1) * gw]
            qs_ref[c, p * SB_PART_ROWS:(p + 1) * SB_PART_ROWS, :] = jnp.concatenate(
                [q_bf * head_keep[h] for h in range(SB_GROUP)], axis=0)

    jj = lax.broadcasted_iota(jnp.int32, (tk, tk), 0)
    ss = lax.broadcasted_iota(jnp.int32, (tk, tk), 1)
    u_mat = jnp.where(jj >= ss, 1.0, 0.0).astype(BF16)

    car_ref[...] = jnp.zeros_like(car_ref)

    def softplus2_bf16(z):
        zb = z.astype(BF16)
        ln_w = jnp.log(1.0 + jnp.exp2(-jnp.abs(zb)))
        return jnp.maximum(zb, 0.0) + (ln_w * LOG2E_BF16_HI + ln_w * LOG2E_BF16_LO)

    def block(kb, band):
        start = pl.multiple_of(kb * tk, tk)
        r0 = 0 if band is None else band * SB_PART_ROWS
        rws = slice(r0, rows)
        for c, (bb, g) in enumerate(SB_CHAINS):
            gcols = slice(g * gw, (g + 1) * gw)
            z = _nt_dot(qs_ref[c, rws, :], k_ref[bb, pl.ds(start, tk), gcols])
            sp = softplus2_bf16(z)
            if band is not None:
                sp = sp * dm_ref[0:rows - r0, :]
            rb = _dot(sp, u_mat)
            arg = z - (rb + car_ref[c, rws, :])
            if band is not None:
                a = jnp.exp2(jnp.minimum(arg, 0.0).astype(BF16)) * dm_ref[0:rows - r0, :]
            else:
                a = jnp.exp2(arg.astype(BF16))
            av = _dot(a, v_ref[bb, pl.ds(start, tk), gcols])
            if band is None:
                acc_ref[c, rws, :] += av
            else:
                r1 = r0 + SB_PART_ROWS
                acc_ref[c, r0:r1, :] = av[0:SB_PART_ROWS]
                if r1 < rows:
                    acc_ref[c, r1:rows, :] += av[SB_PART_ROWS:]
            car_ref[c, rws, :] += rb[:, 0:1]

    first_kb = qi * SB_N_DIAG
    for d in reversed(range(SB_N_DIAG)):
        block(first_kb + d, d)

    @pl.when(qi == 0)
    def _():
        for c, (bb, g) in enumerate(SB_CHAINS):
            kf = k_ref[bb, :, g * gw:(g + 1) * gw].astype(F32)
            kmax2_ref[c] = jnp.max(jnp.sum(kf * kf, axis=-1, keepdims=True))

    zmax2 = []
    for c in range(len(SB_CHAINS)):
        qf = qs_ref[c].astype(F32)
        qmax2 = jnp.max(jnp.sum(qf * qf, axis=-1, keepdims=True))
        zmax2.append(qmax2 * kmax2_ref[c] * SB_BOUND_SLACK)

    def more_to_do():
        go = False
        for c in range(len(SB_CHAINS)):
            spare = jnp.min(car_ref[c]) - SB_UNDERFLOW_BITS
            go = jnp.logical_or(go, jnp.logical_or(spare <= 0.0, spare * spare <= zmax2[c]))
        return go

    def cond(carry):
        i, go = carry
        return jnp.logical_and(i < first_kb, go)

    def body(carry):
        i, _ = carry
        block(first_kb - 1 - i, None)
        return i + 1, more_to_do()

    lax.while_loop(cond, body, (jnp.int32(0), more_to_do()))

    for c, (bb, g) in enumerate(SB_CHAINS):
        for p in range(SB_N_DIAG):
            r = p * SB_PART_ROWS
            o = acc_ref[c, r:r + tk, :]
            for h in range(1, SB_GROUP):
                o = jnp.where(lane_head == h, acc_ref[c, r + h * tk:r + (h + 1) * tk, :], o)
            zg = z_ref[bb, p * tk:(p + 1) * tk, g * gw:(g + 1) * gw].astype(F32)
            o_ref[bb, p * tk:(p + 1) * tk, g * gw:(g + 1) * gw] = (
                o * _silu(zg)).astype(o_ref.dtype)


def _stick_breaking(proj):
    bsz, s, _ = proj.shape
    w = SB_WIDTH
    n_chains = len(SB_CHAINS)
    rows = SB_N_DIAG * SB_PART_ROWS
    return pl.pallas_call(
        _sb_kernel,
        out_shape=jax.ShapeDtypeStruct((bsz, s, w), BF16),
        grid=(bsz // SB_BPS, s // SB_TQ),
        in_specs=[
            pl.BlockSpec((SB_BPS, SB_TQ, w), lambda b, i: (b, i, COL_QA)),
            pl.BlockSpec((SB_BPS, s, w), lambda b, i: (b, 0, COL_KA)),
            pl.BlockSpec((SB_BPS, s, w), lambda b, i: (b, 0, COL_VA)),
            pl.BlockSpec((SB_BPS, SB_TQ, w), lambda b, i: (b, i, COL_ZA)),
            pl.BlockSpec((rows, SB_TK), lambda b, i: (0, 0)),
        ],
        out_specs=pl.BlockSpec((SB_BPS, SB_TQ, w), lambda b, i: (b, i, 0)),
        scratch_shapes=[
            pltpu.VMEM((n_chains, rows, SB_GROUP_W), BF16),
            pltpu.VMEM((n_chains, rows, SB_GROUP_W), F32),
            pltpu.VMEM((n_chains, rows, 1), F32),
            pltpu.SMEM((n_chains,), F32),
        ],
        compiler_params=pltpu.CompilerParams(
            dimension_semantics=("arbitrary", "arbitrary"),
            vmem_limit_bytes=V7X_VMEM_LIMIT_BYTES),
        name="stick_breaking",
    )(proj, proj, proj, proj, jnp.asarray(_sb_band_mask(), BF16))


def _hgrn_sum_matrix():
    c = HG_C
    w = np.zeros((HG_NMAT, c, c), np.float32)
    for i, lvl in enumerate(HG_MM_LEVELS):
        n = 1 << lvl
        for r in range(c):
            m = (r // n) * n + n // 2
            if r >= m:
                w[i, r, m:r + 1] = 1.0
            else:
                w[i, r, r + 1:m] = 1.0
    for r in range(c):
        w[HG_NMAT - 2, r, :r + 1] = 1.0
        w[HG_NMAT - 1, r, r + 1:] = 1.0
    return w.reshape(HG_NMAT * c, c)


def _hgrn_level_matrix():
    c = HG_C
    lv = np.full((c, c), -1, np.int32)
    for t in range(c):
        for s in range(t + 1):
            lv[t, s] = t - s if t - s <= HG_BAND else HG_BAND + (t ^ s).bit_length()
    return lv


def _hgrn_kernel(layer, q_ref, f_ref, i_ref, z_ref, lb_ref, nw_ref, w_ref, lv_ref, o_ref,
                 state_ref):
    c = HG_C
    dk = HG_HEAD_DIM
    ci = pl.program_id(1)

    @pl.when(ci == 0)
    def _():
        state_ref[...] = jnp.zeros_like(state_ref)

    lbw = lb_ref[...]
    e = jnp.exp(lbw - jnp.max(lbw, axis=0, keepdims=True))
    p = e / jnp.sum(e, axis=0, keepdims=True)
    lb_all = jnp.zeros((1, HG_WIDTH), F32)
    for l in range(1, layer + 1):
        lb_all = lb_all + p[l:l + 1]

    lv = lv_ref[...]
    for cc, h in [(cc, h) for cc in range(HG_CPS) for h in range(HG_HEADS)]:
        rws = slice(cc * c, (cc + 1) * c)
        cols = slice(h * dk, (h + 1) * dk)
        if h == 0:
            f_all = lb_all + (1.0 - lb_all) * _sigmoid(f_ref[0, rws, :].astype(F32))
            g_all = jnp.log(f_all) * LOG2E
            g_hi, g_lo = _split_bf16(g_all)
        if h % 2 == 0:
            pair = slice(h * dk, (h + 2) * dk)
            g_cat = jnp.concatenate([g_hi[:, pair], g_lo[:, pair]], axis=0)
            ex_pair = jnp.exp2(_dot(w_ref[...], g_cat))
            ex_pair_bf = ex_pair.astype(BF16)
        half = slice((h % 2) * dk, (h % 2 + 1) * dk)
        ex_bf = ex_pair_bf[:, half]
        f_h = f_all[:, cols]
        k_f = 1.0 - f_h
        q_f = _silu(q_ref[0, rws, cols].astype(F32))
        k_bf = k_f.astype(BF16)
        q_bf = q_f.astype(BF16)
        v_bf = i_ref[0, rws, cols]

        scores = jnp.where(lv == 0, jnp.sum(q_f * k_f, axis=-1, keepdims=True), 0.0)
        qd = q_f
        for j in range(1, HG_BAND + 1):
            qd = qd * (f_h if j == 1 else pltpu.roll(f_h, j - 1, axis=0))
            d_j = jnp.sum(qd * pltpu.roll(k_f, j, axis=0), axis=-1, keepdims=True)
            scores = jnp.where(lv == j, d_j, scores)
        for i, lvl in enumerate(HG_MM_LEVELS):
            dec = ex_bf[i * c:(i + 1) * c]
            scores = jnp.where(lv == HG_BAND + lvl, _nt_dot(q_bf * dec, k_bf * dec), scores)

        dec_q = ex_bf[(HG_NMAT - 2) * c:(HG_NMAT - 1) * c]
        dec_k = ex_bf[(HG_NMAT - 1) * c:HG_NMAT * c]
        state = state_ref[h]
        inter = _nt_dot(q_bf * dec_q, state.astype(BF16))
        intra = _dot(scores.astype(BF16), v_bf)
        o = inter + intra

        v_t = v_bf.astype(F32).T.astype(BF16)
        dec_end = ex_pair[(HG_NMAT - 1) * c - 1:(HG_NMAT - 1) * c, half]
        state_ref[h] = state * dec_end + _dot(v_t, k_bf * dec_k)

        ms = jnp.mean(o * o, axis=-1, keepdims=True)
        o = o * lax.rsqrt(ms + RMS_EPS) * nw_ref[...]
        o_ref[0, rws, cols] = (o * _silu(z_ref[0, rws, cols].astype(F32))).astype(o_ref.dtype)


def _hgrn2(proj, lower_bounds, norm_w_l, layer):
    bsz, s, _ = proj.shape
    depth = lower_bounds.shape[0]
    dk = HG_HEAD_DIM
    w = HG_WIDTH
    w_one = _hgrn_sum_matrix()
    w_sum = jnp.asarray(np.concatenate([w_one, w_one], axis=1), BF16)
    lv = jnp.asarray(_hgrn_level_matrix())
    tr = HG_CPS * HG_C
    return pl.pallas_call(
        partial(_hgrn_kernel, layer),
        out_shape=jax.ShapeDtypeStruct((bsz, s, w), BF16),
        grid=(bsz, s // tr),
        in_specs=[
            pl.BlockSpec((1, tr, w), lambda b, i: (b, i, COL_QB)),
            pl.BlockSpec((1, tr, w), lambda b, i: (b, i, COL_FB)),
            pl.BlockSpec((1, tr, w), lambda b, i: (b, i, COL_IB)),
            pl.BlockSpec((1, tr, w), lambda b, i: (b, i, COL_ZB)),
            pl.BlockSpec((depth, w), lambda b, i: (0, 0)),
            pl.BlockSpec((1, dk), lambda b, i: (0, 0)),
            pl.BlockSpec((HG_NMAT * HG_C, 2 * HG_C), lambda b, i: (0, 0)),
            pl.BlockSpec((HG_C, HG_C), lambda b, i: (0, 0)),
        ],
        out_specs=pl.BlockSpec((1, tr, w), lambda b, i: (b, i, 0)),
        scratch_shapes=[pltpu.VMEM((HG_HEADS, dk, dk), F32)],
        compiler_params=pltpu.CompilerParams(
            dimension_semantics=("arbitrary", "arbitrary"),
            vmem_limit_bytes=V7X_VMEM_LIMIT_BYTES),
        name="hgrn2",
    )(proj, proj, proj, proj, lower_bounds, norm_w_l.reshape(1, dk), w_sum, lv)


def _merge_kernel(alpha, emit_h, ya_ref, yb_ref, pre_ref, post_ref, u_ref, zc_ref,
                  pre_h_ref, u_h_ref, ga_ref, gb_ref, gc_ref, x_ref, gate_ref, cw_ref,
                  wb_ref, wo_ref, lng_ref, lnb_ref, *rest):
    if emit_h:
        nmod_ref, xo_ref, h_ref, ext_ref = rest
    else:
        xo_ref, ext_ref = rest
    i = pl.program_id(1)
    tm = pre_ref.shape[1]

    hp = pre_h_ref[0, HALO_ROWS - SUBLANES:HALO_ROWS, :].astype(F32)
    hu = u_h_ref[0, HALO_ROWS - SUBLANES:HALO_ROWS, :].astype(F32)
    ext_ref[0:SUBLANES, :] = jnp.where(i > 0, hp * hu, 0.0)
    ext_ref[SUBLANES:SUBLANES + tm, :] = pre_ref[0].astype(F32) * u_ref[0].astype(F32)
    cw = cw_ref[...]
    conv = (cw[2:3] * ext_ref[SUBLANES:SUBLANES + tm, :]
            + cw[1:2] * ext_ref[SUBLANES - 1:SUBLANES - 1 + tm, :]
            + cw[0:1] * ext_ref[SUBLANES - 2:SUBLANES - 2 + tm, :])
    y_c = (post_ref[0].astype(F32) * conv * _silu(zc_ref[0].astype(F32))).astype(BF16)

    def gate(ref):
        return 1.0 / (1.0 + jnp.exp2(ref[0].astype(F32)))

    merged = (gate(ga_ref) * _dot(ya_ref[0], wb_ref[0])
              + gate(gb_ref) * _dot(yb_ref[0], wb_ref[1])
              + gate(gc_ref) * _dot(y_c, wb_ref[2]))
    y = _dot(merged.astype(BF16), wo_ref[...])

    r = alpha * x_ref[0] + (1.0 + gate_ref[0]) * y
    x_new = _standardize(r) * lng_ref[...] + lnb_ref[...]
    xo_ref[0] = x_new
    if emit_h:
        d = x_new.shape[-1]
        shift = nmod_ref[0, :, 0:d]
        scale = nmod_ref[0, :, d:2 * d]
        h_ref[0] = (_standardize(x_new) * (1.0 + scale) + shift).astype(h_ref.dtype)


def _merge(ya, yb, proj, x, mod_l, conv_w_l, wb_l, wo_l, lng_l, lnb_l, alpha, next_mod):
    bsz, s, d = x.shape
    tm = MERGE_TM
    cw = CONV_WIDTH
    emit_h = next_mod is not None
    halo_blocks = tm // HALO_ROWS

    def halo_map(col):
        return lambda b, i: (b, jnp.maximum(i * halo_blocks - 1, 0), col)

    in_specs = [
        pl.BlockSpec((1, tm, SB_WIDTH), lambda b, i: (b, i, 0)),
        pl.BlockSpec((1, tm, HG_WIDTH), lambda b, i: (b, i, 0)),
        pl.BlockSpec((1, tm, cw), lambda b, i: (b, i, COL_PRE)),
        pl.BlockSpec((1, tm, cw), lambda b, i: (b, i, COL_POST)),
        pl.BlockSpec((1, tm, cw), lambda b, i: (b, i, COL_U)),
        pl.BlockSpec((1, tm, cw), lambda b, i: (b, i, COL_ZC)),
        pl.BlockSpec((1, HALO_ROWS, cw), halo_map(COL_PRE)),
        pl.BlockSpec((1, HALO_ROWS, cw), halo_map(COL_U)),
        pl.BlockSpec((1, tm, d), lambda b, i: (b, i, COL_GATES)),
        pl.BlockSpec((1, tm, d), lambda b, i: (b, i, COL_GATES + 1)),
        pl.BlockSpec((1, tm, d), lambda b, i: (b, i, COL_GATES + 2)),
        pl.BlockSpec((1, tm, d), lambda b, i: (b, i, 0)),
        pl.BlockSpec((1, 1, d), lambda b, i: (b, 0, 2)),
        pl.BlockSpec((CONV_K, cw), lambda b, i: (0, 0)),
        pl.BlockSpec((3, SB_WIDTH, d), lambda b, i: (0, 0, 0)),
        pl.BlockSpec((d, d), lambda b, i: (0, 0)),
        pl.BlockSpec((1, d), lambda b, i: (0, 0)),
        pl.BlockSpec((1, d), lambda b, i: (0, 0)),
    ]
    args = [ya, yb] + [proj] * 9 + [x, mod_l, conv_w_l, wb_l, wo_l,
            lng_l.reshape(1, d), lnb_l.reshape(1, d)]
    out_shape = [jax.ShapeDtypeStruct((bsz, s, d), F32)]
    out_specs = [pl.BlockSpec((1, tm, d), lambda b, i: (b, i, 0))]
    if emit_h:
        in_specs.append(pl.BlockSpec((1, 1, 3 * d), lambda b, i: (b, 0, 0)))
        args.append(next_mod)
        out_shape.append(jax.ShapeDtypeStruct((bsz, s, d), BF16))
        out_specs.append(pl.BlockSpec((1, tm, d), lambda b, i: (b, i, 0)))
    outs = pl.pallas_call(
        partial(_merge_kernel, alpha, emit_h),
        out_shape=out_shape,
        grid=(bsz, s // tm),
        in_specs=in_specs,
        out_specs=out_specs,
        scratch_shapes=[pltpu.VMEM((tm + SUBLANES, cw), F32)],
        compiler_params=pltpu.CompilerParams(
            dimension_semantics=("arbitrary", "arbitrary"),
            vmem_limit_bytes=V7X_VMEM_LIMIT_BYTES),
        name="merge_residual",
    )(*args)
    return (outs[0], outs[1]) if emit_h else (outs[0], None)


def _in_proj_col_scale():
    cs = np.ones((1, IN_COLS), np.float32)
    cs[:, COL_QA * 512:(COL_QA + 1) * 512] = LOG2E * SB_HEAD_DIM ** -0.5
    cs[:, COL_GATES * D_MODEL:] = -LOG2E
    return jnp.asarray(cs)


def kernel(x, c, w_mod, b_mod, w_in, conv_w, hgrn_norm_w, lower_bounds, w_branch, w_out, ln_g, ln_b):
    bsz, s, d = x.shape
    depth = w_mod.shape[0]
    alpha = (2.0 * depth) ** 0.25

    mod = _modulation(c, w_mod, b_mod).reshape(depth, bsz, 1, 3 * d)
    col_scale = _in_proj_col_scale()

    h = _lnmod(x, mod[0])
    for l in range(depth):
        proj = _in_proj(h.reshape(bsz * s, d), w_in, l, col_scale).reshape(bsz, s, IN_COLS)
        ya = _stick_breaking(proj)
        yb = _hgrn2(proj, lower_bounds, hgrn_norm_w[l], l)
        next_mod = mod[l + 1] if l + 1 < depth else None
        x, h = _merge(ya, yb, proj, x, mod[l], conv_w[l], w_branch[l].astype(BF16),
                      w_out[l].astype(BF16), ln_g[l], ln_b[l], alpha, next_mod)
    return x
```

```python
from functools import partial

import numpy as np
import jax
import jax.numpy as jnp
from jax import lax
from jax.experimental import pallas as pl
from jax.experimental.pallas import tpu as pltpu

D_MODEL = 1024
SB_HEADS = 8
SB_HEAD_DIM = 64
SB_WIDTH = SB_HEADS * SB_HEAD_DIM
HG_HEADS = 4
HG_HEAD_DIM = 128
HG_WIDTH = HG_HEADS * HG_HEAD_DIM
CONV_WIDTH = 512
CONV_K = 3
LN_EPS = 1e-5
RMS_EPS = 1e-6

F32 = jnp.float32
BF16 = jnp.bfloat16

V7X_VMEM_LIMIT_BYTES = 56 * 1024 * 1024
SUBLANES = 8
HALO_ROWS = 16

IN_COLS = 12 * 512 + 3 * D_MODEL
(COL_QA, COL_KA, COL_VA, COL_ZA, COL_QB, COL_FB, COL_IB, COL_ZB,
 COL_PRE, COL_POST, COL_U, COL_ZC) = range(12)
COL_GATES = 12 * 512 // D_MODEL
IN_PROJ_TM = 2048
IN_PROJ_TN = 1536
MERGE_TM = 512
LNMOD_TM = 1024

SB_GROUP = 4
SB_GROUP_W = SB_GROUP * SB_HEAD_DIM
SB_N_GROUPS = SB_HEADS // SB_GROUP
SB_TQ = 512
SB_TK = 256
SB_N_DIAG = SB_TQ // SB_TK
SB_PART_ROWS = SB_GROUP * SB_TK
SB_BPS = 2
SB_UNDERFLOW_BITS = 160.0
SB_BOUND_SLACK = 1.05
SB_CHAINS = [(bb, g) for bb in range(SB_BPS) for g in range(SB_N_GROUPS)]
LOG2E = 1.4426950408889634
LOG2E_BF16_HI = 1.4453125
LOG2E_BF16_LO = -0.00262451171875

HG_C = 128
HG_CPS = 4
HG_LEVELS = 7
HG_BAND = 3
HG_MM_LEVELS = range(3, HG_LEVELS + 1)
HG_NMAT = len(HG_MM_LEVELS) + 2


def _nt_dot(a, b):
    return lax.dot_general(a, b, (((1,), (1,)), ((), ())), preferred_element_type=F32)


def _dot(a, b):
    return jnp.dot(a, b, preferred_element_type=F32)


def _split_bf16(v):
    hi = v.astype(BF16)
    lo = (v - hi.astype(F32)).astype(BF16)
    return hi, lo


def _sigmoid(v):
    return 1.0 / (1.0 + jnp.exp(-v))


def _silu(v):
    return v * _sigmoid(v)


def _standardize(xf):
    mu = jnp.mean(xf, axis=-1, keepdims=True)
    xc = xf - mu
    var = jnp.mean(xc * xc, axis=-1, keepdims=True)
    return xc * lax.rsqrt(var + LN_EPS)


def _mod_kernel(c_ref, w_ref, b_ref, o_ref):
    c_hi, c_lo = _split_bf16(c_ref[...])
    w_hi, w_lo = _split_bf16(w_ref[0])
    acc = _dot(c_hi, w_hi) + (_dot(c_hi, w_lo) + _dot(c_lo, w_hi))
    o_ref[0] = acc + b_ref[0]


def _modulation(c, w_mod, b_mod):
    depth, d, n = w_mod.shape
    bsz = c.shape[0]
    tn = 1024
    return pl.pallas_call(
        _mod_kernel,
        out_shape=jax.ShapeDtypeStruct((depth, bsz, n), F32),
        grid=(depth, n // tn),
        in_specs=[
            pl.BlockSpec((bsz, d), lambda l, j: (0, 0)),
            pl.BlockSpec((1, d, tn), lambda l, j: (l, 0, j)),
            pl.BlockSpec((1, 1, tn), lambda l, j: (l, 0, j)),
        ],
        out_specs=pl.BlockSpec((1, bsz, tn), lambda l, j: (l, 0, j)),
        compiler_params=pltpu.CompilerParams(
            dimension_semantics=("arbitrary", "arbitrary"),
            vmem_limit_bytes=V7X_VMEM_LIMIT_BYTES),
        name="adaln_mod",
    )(c, w_mod, b_mod.reshape(depth, 1, n))


def _lnmod_kernel(x_ref, mod_ref, h_ref):
    d = x_ref.shape[-1]
    shift = mod_ref[0, :, 0:d]
    scale = mod_ref[0, :, d:2 * d]
    h_ref[0] = (_standardize(x_ref[0]) * (1.0 + scale) + shift).astype(h_ref.dtype)


def _lnmod(x, mod_l):
    bsz, s, d = x.shape
    tm = LNMOD_TM
    return pl.pallas_call(
        _lnmod_kernel,
        out_shape=jax.ShapeDtypeStruct((bsz, s, d), BF16),
        grid=(bsz, s // tm),
        in_specs=[
            pl.BlockSpec((1, tm, d), lambda b, i: (b, i, 0)),
            pl.BlockSpec((1, 1, 3 * d), lambda b, i: (b, 0, 0)),
        ],
        out_specs=pl.BlockSpec((1, tm, d), lambda b, i: (b, i, 0)),
        compiler_params=pltpu.CompilerParams(
            dimension_semantics=("arbitrary", "arbitrary"),
            vmem_limit_bytes=V7X_VMEM_LIMIT_BYTES),
        name="ln_modulate",
    )(x, mod_l)


def _in_proj_kernel(h_ref, w_ref, cs_ref, o_ref, wbf_ref):
    @pl.when(pl.program_id(1) == 0)
    def _():
        wbf_ref[...] = (w_ref[0] * cs_ref[...]).astype(BF16)

    o_ref[...] = _dot(h_ref[...], wbf_ref[...]).astype(o_ref.dtype)


def _in_proj(h2, w_in, layer, col_scale):
    m, k = h2.shape
    n = w_in.shape[2]
    tm, tn = IN_PROJ_TM, IN_PROJ_TN
    return pl.pallas_call(
        _in_proj_kernel,
        out_shape=jax.ShapeDtypeStruct((m, n), BF16),
        grid=(n // tn, m // tm),
        in_specs=[
            pl.BlockSpec((tm, k), lambda j, i: (i, 0)),
            pl.BlockSpec((1, k, tn), lambda j, i: (layer, 0, j)),
            pl.BlockSpec((1, tn), lambda j, i: (0, j)),
        ],
        out_specs=pl.BlockSpec((tm, tn), lambda j, i: (i, j)),
        scratch_shapes=[pltpu.VMEM((k, tn), BF16)],
        compiler_params=pltpu.CompilerParams(
            dimension_semantics=("arbitrary", "arbitrary"),
            vmem_limit_bytes=V7X_VMEM_LIMIT_BYTES),
        name="in_proj",
    )(h2, w_in, col_scale)


def _sb_band_mask():
    t_in = np.arange(SB_PART_ROWS)[:, None] % SB_TK
    tri = np.arange(SB_TK)[None, :] < t_in
    ones = np.ones(((SB_N_DIAG - 1) * SB_PART_ROWS, SB_TK), bool)
    return np.concatenate([tri, ones], axis=0).astype(np.float32)


def _sb_kernel(q_ref, k_ref, v_ref, z_ref, dm_ref, o_ref, qs_ref, acc_ref, car_ref, kmax2_ref,
               cmin_ref):
    qi = pl.program_id(1)
    tk, gw = SB_TK, SB_GROUP_W
    rows = SB_N_DIAG * SB_PART_ROWS

    lane_head = lax.broadcasted_iota(jnp.int32, (tk, gw), 1) // SB_HEAD_DIM
    head_row = lax.broadcasted_iota(jnp.int32, (1, gw), 1) // SB_HEAD_DIM
    head_keep = [jnp.where(head_row == h, 1.0, 0.0).astype(BF16) for h in range(SB_GROUP)]
    for c, (bb, g) in enumerate(SB_CHAINS):
        for p in range(SB_N_DIAG):
            q_bf = q_ref[bb, p * tk:(p + 1) * tk, g * gw:(g + 1) * gw]
            qs_ref[c, p * SB_PART_ROWS:(p + 1) * SB_PART_ROWS, :] = jnp.concatenate(
                [q_bf * head_keep[h] for h in range(SB_GROUP)], axis=0)

    jj = lax.broadcasted_iota(jnp.int32, (tk, tk), 0)
    ss = lax.broadcasted_iota(jnp.int32, (tk, tk), 1)
    u_mat = jnp.where(jj >= ss, 1.0, 0.0).astype(BF16)

    car_ref[...] = jnp.zeros_like(car_ref)

    def softplus2_bf16(z):
        zb = z.astype(BF16)
        ln_w = jnp.log(1.0 + jnp.exp2(-jnp.abs(zb)))
        return jnp.maximum(zb, 0.0) + (ln_w * LOG2E_BF16_HI + ln_w * LOG2E_BF16_LO)

    def block(kb, band, r_hi=rows):
        start = pl.multiple_of(kb * tk, tk)
        r0 = 0 if band is None else band * SB_PART_ROWS
        rws = slice(r0, r_hi)
        for c, (bb, g) in enumerate(SB_CHAINS):
            gcols = slice(g * gw, (g + 1) * gw)
            z = _nt_dot(qs_ref[c, rws, :], k_ref[bb, pl.ds(start, tk), gcols])
            sp = softplus2_bf16(z)
            if band is not None:
                sp = sp * dm_ref[0:rows - r0, :]
            rb = _dot(sp, u_mat)
            arg = z - (rb + car_ref[c, rws, :])
            if band is not None:
                a = jnp.exp2(jnp.minimum(arg, 0.0).astype(BF16)) * dm_ref[0:rows - r0, :]
            else:
                a = jnp.exp2(arg.astype(BF16))
            av = _dot(a, v_ref[bb, pl.ds(start, tk), gcols])
            if band is None:
                acc_ref[c, rws, :] += av
            else:
                r1 = r0 + SB_PART_ROWS
                acc_ref[c, r0:r1, :] = av[0:SB_PART_ROWS]
                if r1 < rows:
                    acc_ref[c, r1:rows, :] += av[SB_PART_ROWS:]
            new_car = car_ref[c, rws, :] + rb[:, 0:1]
            car_ref[c, rws, :] = new_car
            for p in range(r0 // SB_PART_ROWS, r_hi // SB_PART_ROWS):
                lo = p * SB_PART_ROWS - r0
                cmin_ref[c * SB_N_DIAG + p] = jnp.min(new_car[lo:lo + SB_PART_ROWS])

    first_kb = qi * SB_N_DIAG
    for d in reversed(range(SB_N_DIAG)):
        block(first_kb + d, d)

    @pl.when(qi == 0)
    def _():
        for c, (bb, g) in enumerate(SB_CHAINS):
            kf = k_ref[bb, :, g * gw:(g + 1) * gw].astype(F32)
            kmax2_ref[c] = jnp.max(jnp.sum(kf * kf, axis=-1, keepdims=True))

    finished_at = []
    for c in range(len(SB_CHAINS)):
        qf = qs_ref[c].astype(F32)
        qmax2 = jnp.max(jnp.sum(qf * qf, axis=-1, keepdims=True))
        zmax = jnp.sqrt(jnp.full((SUBLANES, 128), qmax2 * kmax2_ref[c] * SB_BOUND_SLACK, F32))
        finished_at.append(jnp.max(zmax) + SB_UNDERFLOW_BITS)

    def flags():
        go_head = jnp.bool_(False)
        go_tail = jnp.bool_(False)
        for c in range(len(SB_CHAINS)):
            go_head = jnp.logical_or(go_head, cmin_ref[c * SB_N_DIAG] <= finished_at[c])
            for p in range(1, SB_N_DIAG):
                go_tail = jnp.logical_or(go_tail, cmin_ref[c * SB_N_DIAG + p] <= finished_at[c])
        return go_head, go_tail

    def cond(carry):
        i, go_head, go_tail = carry
        return jnp.logical_and(i < first_kb, jnp.logical_or(go_head, go_tail))

    def body(carry):
        i, _, go_tail = carry
        kb = first_kb - 1 - i

        @pl.when(go_tail)
        def _():
            block(kb, None)

        @pl.when(jnp.logical_not(go_tail))
        def _():
            block(kb, None, SB_PART_ROWS)

        return (i + 1,) + flags()

    lax.while_loop(cond, body, (jnp.int32(0),) + flags())

    for c, (bb, g) in enumerate(SB_CHAINS):
        for p in range(SB_N_DIAG):
            r = p * SB_PART_ROWS
            o = acc_ref[c, r:r + tk, :]
            for h in range(1, SB_GROUP):
                o = jnp.where(lane_head == h, acc_ref[c, r + h * tk:r + (h + 1) * tk, :], o)
            zg = z_ref[bb, p * tk:(p + 1) * tk, g * gw:(g + 1) * gw].astype(F32)
            o_ref[bb, p * tk:(p + 1) * tk, g * gw:(g + 1) * gw] = (
                o * _silu(zg)).astype(o_ref.dtype)


def _stick_breaking(proj):
    bsz, s, _ = proj.shape
    w = SB_WIDTH
    n_chains = len(SB_CHAINS)
    rows = SB_N_DIAG * SB_PART_ROWS
    return pl.pallas_call(
        _sb_kernel,
        out_shape=jax.ShapeDtypeStruct((bsz, s, w), BF16),
        grid=(bsz // SB_BPS, s // SB_TQ),
        in_specs=[
            pl.BlockSpec((SB_BPS, SB_TQ, w), lambda b, i: (b, i, COL_QA)),
            pl.BlockSpec((SB_BPS, s, w), lambda b, i: (b, 0, COL_KA)),
            pl.BlockSpec((SB_BPS, s, w), lambda b, i: (b, 0, COL_VA)),
            pl.BlockSpec((SB_BPS, SB_TQ, w), lambda b, i: (b, i, COL_ZA)),
            pl.BlockSpec((rows, SB_TK), lambda b, i: (0, 0)),
        ],
        out_specs=pl.BlockSpec((SB_BPS, SB_TQ, w), lambda b, i: (b, i, 0)),
        scratch_shapes=[
            pltpu.VMEM((n_chains, rows, SB_GROUP_W), BF16),
            pltpu.VMEM((n_chains, rows, SB_GROUP_W), F32),
            pltpu.VMEM((n_chains, rows, 1), F32),
            pltpu.SMEM((n_chains,), F32),
            pltpu.SMEM((n_chains * SB_N_DIAG,), F32),
        ],
        compiler_params=pltpu.CompilerParams(
            dimension_semantics=("arbitrary", "arbitrary"),
            vmem_limit_bytes=V7X_VMEM_LIMIT_BYTES),
        name="stick_breaking",
    )(proj, proj, proj, proj, jnp.asarray(_sb_band_mask(), BF16))


def _hgrn_sum_matrix():
    c = HG_C
    w = np.zeros((HG_NMAT, c, c), np.float32)
    for i, lvl in enumerate(HG_MM_LEVELS):
        n = 1 << lvl
        for r in range(c):
            m = (r // n) * n + n // 2
            if r >= m:
                w[i, r, m:r + 1] = 1.0
            else:
                w[i, r, r + 1:m] = 1.0
    for r in range(c):
        w[HG_NMAT - 2, r, :r + 1] = 1.0
        w[HG_NMAT - 1, r, r + 1:] = 1.0
    return w.reshape(HG_NMAT * c, c)


def _hgrn_level_matrix():
    c = HG_C
    lv = np.full((c, c), -1, np.int32)
    for t in range(c):
        for s in range(t + 1):
            lv[t, s] = t - s if t - s <= HG_BAND else HG_BAND + (t ^ s).bit_length()
    return lv


def _hgrn_kernel(layer, q_ref, f_ref, i_ref, z_ref, lb_ref, nw_ref, w_ref, lv_ref, o_ref,
                 state_ref):
    c = HG_C
    dk = HG_HEAD_DIM
    ci = pl.program_id(1)

    @pl.when(ci == 0)
    def _():
        state_ref[...] = jnp.zeros_like(state_ref)

    lbw = lb_ref[...]
    e = jnp.exp(lbw - jnp.max(lbw, axis=0, keepdims=True))
    p = e / jnp.sum(e, axis=0, keepdims=True)
    lb_all = jnp.zeros((1, HG_WIDTH), F32)
    for l in range(1, layer + 1):
        lb_all = lb_all + p[l:l + 1]

    lv = lv_ref[...]
    for cc, h in [(cc, h) for cc in range(HG_CPS) for h in range(HG_HEADS)]:
        rws = slice(cc * c, (cc + 1) * c)
        cols = slice(h * dk, (h + 1) * dk)
        if h == 0:
            f_all = lb_all + (1.0 - lb_all) * _sigmoid(f_ref[0, rws, :].astype(F32))
            g_all = jnp.log(f_all) * LOG2E
            g_hi, g_lo = _split_bf16(g_all)
        if h % 2 == 0:
            pair = slice(h * dk, (h + 2) * dk)
            g_cat = jnp.concatenate([g_hi[:, pair], g_lo[:, pair]], axis=0)
            ex_pair = jnp.exp2(_dot(w_ref[...], g_cat))
            ex_pair_bf = ex_pair.astype(BF16)
        half = slice((h % 2) * dk, (h % 2 + 1) * dk)
        ex_bf = ex_pair_bf[:, half]
        f_h = f_all[:, cols]
        k_f = 1.0 - f_h
        q_f = _silu(q_ref[0, rws, cols].astype(F32))
        k_bf = k_f.astype(BF16)
        q_bf = q_f.astype(BF16)
        v_bf = i_ref[0, rws, cols]

        scores = jnp.where(lv == 0, jnp.sum(q_f * k_f, axis=-1, keepdims=True), 0.0)
        qd = q_f
        for j in range(1, HG_BAND + 1):
            qd = qd * (f_h if j == 1 else pltpu.roll(f_h, j - 1, axis=0))
            d_j = jnp.sum(qd * pltpu.roll(k_f, j, axis=0), axis=-1, keepdims=True)
            scores = jnp.where(lv == j, d_j, scores)
        for i, lvl in enumerate(HG_MM_LEVELS):
            dec = ex_bf[i * c:(i + 1) * c]
            scores = jnp.where(lv == HG_BAND + lvl, _nt_dot(q_bf * dec, k_bf * dec), scores)

        dec_q = ex_bf[(HG_NMAT - 2) * c:(HG_NMAT - 1) * c]
        dec_k = ex_bf[(HG_NMAT - 1) * c:HG_NMAT * c]
        state = state_ref[h]
        inter = _nt_dot(q_bf * dec_q, state.astype(BF16))
        intra = _dot(scores.astype(BF16), v_bf)
        o = inter + intra

        v_t = v_bf.astype(F32).T.astype(BF16)
        dec_end = ex_pair[(HG_NMAT - 1) * c - 1:(HG_NMAT - 1) * c, half]
        state_ref[h] = state * dec_end + _dot(v_t, k_bf * dec_k)

        ms = jnp.mean(o * o, axis=-1, keepdims=True)
        o = o * lax.rsqrt(ms + RMS_EPS) * nw_ref[...]
        o_ref[0, rws, cols] = (o * _silu(z_ref[0, rws, cols].astype(F32))).astype(o_ref.dtype)


def _hgrn2(proj, lower_bounds, norm_w_l, layer):
    bsz, s, _ = proj.shape
    depth = lower_bounds.shape[0]
    dk = HG_HEAD_DIM
    w = HG_WIDTH
    w_one = _hgrn_sum_matrix()
    w_sum = jnp.asarray(np.concatenate([w_one, w_one], axis=1), BF16)
    lv = jnp.asarray(_hgrn_level_matrix())
    tr = HG_CPS * HG_C
    return pl.pallas_call(
        partial(_hgrn_kernel, layer),
        out_shape=jax.ShapeDtypeStruct((bsz, s, w), BF16),
        grid=(bsz, s // tr),
        in_specs=[
            pl.BlockSpec((1, tr, w), lambda b, i: (b, i, COL_QB)),
            pl.BlockSpec((1, tr, w), lambda b, i: (b, i, COL_FB)),
            pl.BlockSpec((1, tr, w), lambda b, i: (b, i, COL_IB)),
            pl.BlockSpec((1, tr, w), lambda b, i: (b, i, COL_ZB)),
            pl.BlockSpec((depth, w), lambda b, i: (0, 0)),
            pl.BlockSpec((1, dk), lambda b, i: (0, 0)),
            pl.BlockSpec((HG_NMAT * HG_C, 2 * HG_C), lambda b, i: (0, 0)),
            pl.BlockSpec((HG_C, HG_C), lambda b, i: (0, 0)),
        ],
        out_specs=pl.BlockSpec((1, tr, w), lambda b, i: (b, i, 0)),
        scratch_shapes=[pltpu.VMEM((HG_HEADS, dk, dk), F32)],
        compiler_params=pltpu.CompilerParams(
            dimension_semantics=("arbitrary", "arbitrary"),
            vmem_limit_bytes=V7X_VMEM_LIMIT_BYTES),
        name="hgrn2",
    )(proj, proj, proj, proj, lower_bounds, norm_w_l.reshape(1, dk), w_sum, lv)


def _merge_kernel(alpha, emit_h, ya_ref, yb_ref, pre_ref, post_ref, u_ref, zc_ref,
                  pre_h_ref, u_h_ref, ga_ref, gb_ref, gc_ref, x_ref, gate_ref, cw_ref,
                  wb_ref, wo_ref, lng_ref, lnb_ref, *rest):
    if emit_h:
        nmod_ref, xo_ref, h_ref, ext_ref = rest
    else:
        xo_ref, ext_ref = rest
    i = pl.program_id(1)
    tm = pre_ref.shape[1]

    hp = pre_h_ref[0, HALO_ROWS - SUBLANES:HALO_ROWS, :].astype(F32)
    hu = u_h_ref[0, HALO_ROWS - SUBLANES:HALO_ROWS, :].astype(F32)
    ext_ref[0:SUBLANES, :] = jnp.where(i > 0, hp * hu, 0.0)
    ext_ref[SUBLANES:SUBLANES + tm, :] = pre_ref[0].astype(F32) * u_ref[0].astype(F32)
    cw = cw_ref[...]
    conv = (cw[2:3] * ext_ref[SUBLANES:SUBLANES + tm, :]
            + cw[1:2] * ext_ref[SUBLANES - 1:SUBLANES - 1 + tm, :]
            + cw[0:1] * ext_ref[SUBLANES - 2:SUBLANES - 2 + tm, :])
    y_c = (post_ref[0].astype(F32) * conv * _silu(zc_ref[0].astype(F32))).astype(BF16)

    def gate(ref):
        return 1.0 / (1.0 + jnp.exp2(ref[0].astype(F32)))

    merged = (gate(ga_ref) * _dot(ya_ref[0], wb_ref[0])
              + gate(gb_ref) * _dot(yb_ref[0], wb_ref[1])
              + gate(gc_ref) * _dot(y_c, wb_ref[2]))
    y = _dot(merged.astype(BF16), wo_ref[...])

    r = alpha * x_ref[0] + (1.0 + gate_ref[0]) * y
    x_new = _standardize(r) * lng_ref[...] + lnb_ref[...]
    xo_ref[0] = x_new
    if emit_h:
        d = x_new.shape[-1]
        shift = nmod_ref[0, :, 0:d]
        scale = nmod_ref[0, :, d:2 * d]
        h_ref[0] = (_standardize(x_new) * (1.0 + scale) + shift).astype(h_ref.dtype)


def _merge(ya, yb, proj, x, mod_l, conv_w_l, wb_l, wo_l, lng_l, lnb_l, alpha, next_mod):
    bsz, s, d = x.shape
    tm = MERGE_TM
    cw = CONV_WIDTH
    emit_h = next_mod is not None
    halo_blocks = tm // HALO_ROWS

    def halo_map(col):
        return lambda b, i: (b, jnp.maximum(i * halo_blocks - 1, 0), col)

    in_specs = [
        pl.BlockSpec((1, tm, SB_WIDTH), lambda b, i: (b, i, 0)),
        pl.BlockSpec((1, tm, HG_WIDTH), lambda b, i: (b, i, 0)),
        pl.BlockSpec((1, tm, cw), lambda b, i: (b, i, COL_PRE)),
        pl.BlockSpec((1, tm, cw), lambda b, i: (b, i, COL_POST)),
        pl.BlockSpec((1, tm, cw), lambda b, i: (b, i, COL_U)),
        pl.BlockSpec((1, tm, cw), lambda b, i: (b, i, COL_ZC)),
        pl.BlockSpec((1, HALO_ROWS, cw), halo_map(COL_PRE)),
        pl.BlockSpec((1, HALO_ROWS, cw), halo_map(COL_U)),
        pl.BlockSpec((1, tm, d), lambda b, i: (b, i, COL_GATES)),
        pl.BlockSpec((1, tm, d), lambda b, i: (b, i, COL_GATES + 1)),
        pl.BlockSpec((1, tm, d), lambda b, i: (b, i, COL_GATES + 2)),
        pl.BlockSpec((1, tm, d), lambda b, i: (b, i, 0)),
        pl.BlockSpec((1, 1, d), lambda b, i: (b, 0, 2)),
        pl.BlockSpec((CONV_K, cw), lambda b, i: (0, 0)),
        pl.BlockSpec((3, SB_WIDTH, d), lambda b, i: (0, 0, 0)),
        pl.BlockSpec((d, d), lambda b, i: (0, 0)),
        pl.BlockSpec((1, d), lambda b, i: (0, 0)),
        pl.BlockSpec((1, d), lambda b, i: (0, 0)),
    ]
    args = [ya, yb] + [proj] * 9 + [x, mod_l, conv_w_l, wb_l, wo_l,
            lng_l.reshape(1, d), lnb_l.reshape(1, d)]
    out_shape = [jax.ShapeDtypeStruct((bsz, s, d), F32)]
    out_specs = [pl.BlockSpec((1, tm, d), lambda b, i: (b, i, 0))]
    if emit_h:
        in_specs.append(pl.BlockSpec((1, 1, 3 * d), lambda b, i: (b, 0, 0)))
        args.append(next_mod)
        out_shape.append(jax.ShapeDtypeStruct((bsz, s, d), BF16))
        out_specs.append(pl.BlockSpec((1, tm, d), lambda b, i: (b, i, 0)))
    outs = pl.pallas_call(
        partial(_merge_kernel, alpha, emit_h),
        out_shape=out_shape,
        grid=(bsz, s // tm),
        in_specs=in_specs,
        out_specs=out_specs,
        scratch_shapes=[pltpu.VMEM((tm + SUBLANES, cw), F32)],
        compiler_params=pltpu.CompilerParams(
            dimension_semantics=("arbitrary", "arbitrary"),
            vmem_limit_bytes=V7X_VMEM_LIMIT_BYTES),
        name="merge_residual",
    )(*args)
    return (outs[0], outs[1]) if emit_h else (outs[0], None)


def _in_proj_col_scale():
    cs = np.ones((1, IN_COLS), np.float32)
    cs[:, COL_QA * 512:(COL_QA + 1) * 512] = LOG2E * SB_HEAD_DIM ** -0.5
    cs[:, COL_GATES * D_MODEL:] = -LOG2E
    return jnp.asarray(cs)


def kernel(x, c, w_mod, b_mod, w_in, conv_w, hgrn_norm_w, lower_bounds, w_branch, w_out, ln_g, ln_b):
    bsz, s, d = x.shape
    depth = w_mod.shape[0]
    alpha = (2.0 * depth) ** 0.25

    mod = _modulation(c, w_mod, b_mod).reshape(depth, bsz, 1, 3 * d)
    col_scale = _in_proj_col_scale()

    h = _lnmod(x, mod[0])
    for l in range(depth):
        proj = _in_proj(h.reshape(bsz * s, d), w_in, l, col_scale).reshape(bsz, s, IN_COLS)
        ya = _stick_breaking(proj)
        yb = _hgrn2(proj, lower_bounds, hgrn_norm_w[l], l)
        next_mod = mod[l + 1] if l + 1 < depth else None
        x, h = _merge(ya, yb, proj, x, mod[l], conv_w[l], w_branch[l].astype(BF16),
                      w_out[l].astype(BF16), ln_g[l], ln_b[l], alpha, next_mod)
    return x
```

```python
from functools import partial

import numpy as np
import jax
import jax.numpy as jnp
from jax import lax
from jax.experimental import pallas as pl
from jax.experimental.pallas import tpu as pltpu

D_MODEL = 1024
SB_HEADS = 8
SB_HEAD_DIM = 64
SB_WIDTH = SB_HEADS * SB_HEAD_DIM
HG_HEADS = 4
HG_HEAD_DIM = 128
HG_WIDTH = HG_HEADS * HG_HEAD_DIM
CONV_WIDTH = 512
CONV_K = 3
LN_EPS = 1e-5
RMS_EPS = 1e-6

F32 = jnp.float32
BF16 = jnp.bfloat16

V7X_VMEM_LIMIT_BYTES = 56 * 1024 * 1024
SUBLANES = 8
LANES = 128
HALO_ROWS = 16

COL_GROUP = SB_WIDTH
N_COL_GROUPS = 12
IN_COLS = N_COL_GROUPS * COL_GROUP + 3 * D_MODEL
(COL_QA, COL_KA, COL_VA, COL_ZA, COL_QB, COL_FB, COL_IB, COL_ZB,
 COL_PRE, COL_POST, COL_U, COL_ZC) = range(N_COL_GROUPS)
COL_GATES = N_COL_GROUPS * COL_GROUP // D_MODEL
IN_PROJ_TM = 2048
IN_PROJ_TN = 1536
MERGE_TM = 512
LNMOD_TM = 2048

SB_GROUP = 4
SB_GROUP_W = SB_GROUP * SB_HEAD_DIM
SB_N_GROUPS = SB_HEADS // SB_GROUP
SB_TQ = 512
SB_TK = 256
SB_N_DIAG = SB_TQ // SB_TK
SB_PART_ROWS = SB_GROUP * SB_TK
SB_BPS = 2
SB_UNDERFLOW_BITS = 160.0
SB_BOUND_SLACK = 1.05
SB_CHAINS = [(bb, g) for bb in range(SB_BPS) for g in range(SB_N_GROUPS)]
LOG2E = 1.4426950408889634
LOG2E_BF16_HI = 1.4453125
LOG2E_BF16_LO = -0.00262451171875

HG_C = 128
HG_CPS = 4
HG_LEVELS = 7
HG_BAND = 3
HG_MM_LEVELS = range(3, HG_LEVELS + 1)
HG_NMAT = len(HG_MM_LEVELS) + 2


def _nt_dot(a, b):
    return lax.dot_general(a, b, (((1,), (1,)), ((), ())), preferred_element_type=F32)


def _dot(a, b):
    return jnp.dot(a, b, preferred_element_type=F32)


def _split_bf16(v):
    hi = v.astype(BF16)
    lo = (v - hi.astype(F32)).astype(BF16)
    return hi, lo


def _sigmoid(v):
    return 1.0 / (1.0 + jnp.exp(-v))


def _silu(v):
    return v * _sigmoid(v)


def _standardize(xf):
    mu = jnp.mean(xf, axis=-1, keepdims=True)
    xc = xf - mu
    var = jnp.mean(xc * xc, axis=-1, keepdims=True)
    return xc * lax.rsqrt(var + LN_EPS)


def _mod_kernel(c_ref, w_ref, b_ref, o_ref):
    c_hi, c_lo = _split_bf16(c_ref[...])
    w_hi, w_lo = _split_bf16(w_ref[0])
    acc = _dot(c_hi, w_hi) + (_dot(c_hi, w_lo) + _dot(c_lo, w_hi))
    o_ref[0] = acc + b_ref[0]


def _modulation(c, w_mod, b_mod):
    depth, d, n = w_mod.shape
    bsz = c.shape[0]
    tn = 1024
    return pl.pallas_call(
        _mod_kernel,
        out_shape=jax.ShapeDtypeStruct((depth, bsz, n), F32),
        grid=(depth, n // tn),
        in_specs=[
            pl.BlockSpec((bsz, d), lambda l, j: (0, 0)),
            pl.BlockSpec((1, d, tn), lambda l, j: (l, 0, j)),
            pl.BlockSpec((1, 1, tn), lambda l, j: (l, 0, j)),
        ],
        out_specs=pl.BlockSpec((1, bsz, tn), lambda l, j: (l, 0, j)),
        compiler_params=pltpu.CompilerParams(
            dimension_semantics=("arbitrary", "arbitrary"),
            vmem_limit_bytes=V7X_VMEM_LIMIT_BYTES),
        name="adaln_mod",
    )(c, w_mod, b_mod.reshape(depth, 1, n))


def _lnmod_kernel(x_ref, mod_ref, h_ref):
    d = x_ref.shape[-1]
    shift = mod_ref[0, :, 0:d]
    scale = mod_ref[0, :, d:2 * d]
    h_ref[0] = (_standardize(x_ref[0]) * (1.0 + scale) + shift).astype(h_ref.dtype)


def _lnmod(x, mod_l):
    bsz, s, d = x.shape
    tm = LNMOD_TM
    return pl.pallas_call(
        _lnmod_kernel,
        out_shape=jax.ShapeDtypeStruct((bsz, s, d), BF16),
        grid=(bsz, s // tm),
        in_specs=[
            pl.BlockSpec((1, tm, d), lambda b, i: (b, i, 0)),
            pl.BlockSpec((1, 1, 3 * d), lambda b, i: (b, 0, 0)),
        ],
        out_specs=pl.BlockSpec((1, tm, d), lambda b, i: (b, i, 0)),
        compiler_params=pltpu.CompilerParams(
            dimension_semantics=("arbitrary", "arbitrary"),
            vmem_limit_bytes=V7X_VMEM_LIMIT_BYTES),
        name="ln_modulate",
    )(x, mod_l)


def _in_proj_kernel(h_ref, w_ref, cs_ref, o_ref, wbf_ref):
    @pl.when(pl.program_id(1) == 0)
    def _():
        wbf_ref[...] = (w_ref[0] * cs_ref[...]).astype(BF16)

    o_ref[...] = _dot(h_ref[...], wbf_ref[...]).astype(o_ref.dtype)


def _in_proj(h2, w_in, layer, col_scale):
    m, k = h2.shape
    n = w_in.shape[2]
    tm, tn = IN_PROJ_TM, IN_PROJ_TN
    return pl.pallas_call(
        _in_proj_kernel,
        out_shape=jax.ShapeDtypeStruct((m, n), BF16),
        grid=(n // tn, m // tm),
        in_specs=[
            pl.BlockSpec((tm, k), lambda j, i: (i, 0)),
            pl.BlockSpec((1, k, tn), lambda j, i: (layer, 0, j)),
            pl.BlockSpec((1, tn), lambda j, i: (0, j)),
        ],
        out_specs=pl.BlockSpec((tm, tn), lambda j, i: (i, j)),
        scratch_shapes=[pltpu.VMEM((k, tn), BF16)],
        compiler_params=pltpu.CompilerParams(
            dimension_semantics=("arbitrary", "arbitrary"),
            vmem_limit_bytes=V7X_VMEM_LIMIT_BYTES),
        name="in_proj",
    )(h2, w_in, col_scale)


def _sb_band_mask():
    t_in = np.arange(SB_PART_ROWS)[:, None] % SB_TK
    tri = np.arange(SB_TK)[None, :] < t_in
    ones = np.ones(((SB_N_DIAG - 1) * SB_PART_ROWS, SB_TK), bool)
    return np.concatenate([tri, ones], axis=0).astype(np.float32)


def _sb_kernel(q_ref, k_ref, v_ref, z_ref, dm_ref, o_ref, qs_ref, acc_ref, car_ref, kmax2_ref,
               cmin_ref):
    qi = pl.program_id(1)
    tk, gw = SB_TK, SB_GROUP_W
    rows = SB_N_DIAG * SB_PART_ROWS

    lane_head = lax.broadcasted_iota(jnp.int32, (tk, gw), 1) // SB_HEAD_DIM
    head_row = lax.broadcasted_iota(jnp.int32, (1, gw), 1) // SB_HEAD_DIM
    head_keep = [jnp.where(head_row == h, 1.0, 0.0).astype(BF16) for h in range(SB_GROUP)]
    for c, (bb, g) in enumerate(SB_CHAINS):
        for p in range(SB_N_DIAG):
            q_bf = q_ref[bb, p * tk:(p + 1) * tk, g * gw:(g + 1) * gw]
            qs_ref[c, p * SB_PART_ROWS:(p + 1) * SB_PART_ROWS, :] = jnp.concatenate(
                [q_bf * head_keep[h] for h in range(SB_GROUP)], axis=0)

    jj = lax.broadcasted_iota(jnp.int32, (tk, tk), 0)
    ss = lax.broadcasted_iota(jnp.int32, (tk, tk), 1)
    u_mat = jnp.where(jj >= ss, 1.0, 0.0).astype(BF16)

    def softplus2_bf16(z):
        zb = z.astype(BF16)
        ln_w = jnp.log(1.0 + jnp.exp2(-jnp.abs(zb)))
        return jnp.maximum(zb, 0.0) + (ln_w * LOG2E_BF16_HI + ln_w * LOG2E_BF16_LO)

    def block(kb, band, r_hi=rows):
        start = pl.multiple_of(kb * tk, tk)
        r0 = 0 if band is None else band * SB_PART_ROWS
        rws = slice(r0, r_hi)
        for c, (bb, g) in enumerate(SB_CHAINS):
            gcols = slice(g * gw, (g + 1) * gw)
            z = _nt_dot(qs_ref[c, rws, :], k_ref[bb, pl.ds(start, tk), gcols])
            sp = softplus2_bf16(z)
            if band is not None:
                sp = sp * dm_ref[0:rows - r0, :]
            rb = _dot(sp, u_mat)
            if band is None:
                car = car_ref[c, rws, :]
            else:
                r1 = r0 + SB_PART_ROWS
                car = jnp.zeros((SB_PART_ROWS, 1), F32)
                if r1 < rows:
                    car = jnp.concatenate([car, car_ref[c, r1:rows, :]], axis=0)
            arg = z - (rb + car)
            if band is not None:
                a = jnp.exp2(jnp.minimum(arg, 0.0).astype(BF16)) * dm_ref[0:rows - r0, :]
            else:
                a = jnp.exp2(arg.astype(BF16))
            av = _dot(a, v_ref[bb, pl.ds(start, tk), gcols])
            if band is None:
                acc_ref[c, rws, :] += av
            else:
                acc_ref[c, r0:r1, :] = av[0:SB_PART_ROWS]
                if r1 < rows:
                    acc_ref[c, r1:rows, :] += av[SB_PART_ROWS:]
            new_car = car + rb[:, 0:1]
            car_ref[c, rws, :] = new_car
            for p in range(r0 // SB_PART_ROWS, r_hi // SB_PART_ROWS):
                lo = p * SB_PART_ROWS - r0
                cmin_ref[c * SB_N_DIAG + p] = jnp.min(new_car[lo:lo + SB_PART_ROWS])

    first_kb = qi * SB_N_DIAG
    for d in reversed(range(SB_N_DIAG)):
        block(first_kb + d, d)

    @pl.when(qi == 0)
    def _():
        for c, (bb, g) in enumerate(SB_CHAINS):
            kf = k_ref[bb, :, g * gw:(g + 1) * gw].astype(F32)
            kmax2_ref[c] = jnp.max(jnp.sum(kf * kf, axis=-1, keepdims=True))

    seg_l = lax.broadcasted_iota(jnp.int32, (gw, LANES), 0) // SB_HEAD_DIM
    seg_j = lax.broadcasted_iota(jnp.int32, (gw, LANES), 1)
    seg = jnp.where(seg_l == seg_j, 1.0, 0.0).astype(BF16)
    finished_at = []
    for c, (bb, g) in enumerate(SB_CHAINS):
        qf = q_ref[bb, :, g * gw:(g + 1) * gw].astype(F32)
        qmax2 = jnp.max(_dot((qf * qf).astype(BF16), seg))
        zmax = jnp.sqrt(jnp.full((SUBLANES, LANES), qmax2 * kmax2_ref[c] * SB_BOUND_SLACK, F32))
        finished_at.append(jnp.max(zmax) + SB_UNDERFLOW_BITS)

    def flags():
        go_head = jnp.bool_(False)
        go_tail = jnp.bool_(False)
        for c in range(len(SB_CHAINS)):
            go_head = jnp.logical_or(go_head, cmin_ref[c * SB_N_DIAG] <= finished_at[c])
            for p in range(1, SB_N_DIAG):
                go_tail = jnp.logical_or(go_tail, cmin_ref[c * SB_N_DIAG + p] <= finished_at[c])
        return go_head, go_tail

    def cond(carry):
        i, go_head, go_tail = carry
        return jnp.logical_and(i < first_kb, jnp.logical_or(go_head, go_tail))

    def body(carry):
        i, _, go_tail = carry
        kb = first_kb - 1 - i

        @pl.when(go_tail)
        def _():
            block(kb, None)

        @pl.when(jnp.logical_not(go_tail))
        def _():
            block(kb, None, SB_PART_ROWS)

        return (i + 1,) + flags()

    lax.while_loop(cond, body, (jnp.int32(0),) + flags())

    for c, (bb, g) in enumerate(SB_CHAINS):
        for p in range(SB_N_DIAG):
            r = p * SB_PART_ROWS
            o = acc_ref[c, r:r + tk, :]
            for h in range(1, SB_GROUP):
                o = jnp.where(lane_head == h, acc_ref[c, r + h * tk:r + (h + 1) * tk, :], o)
            zg = z_ref[bb, p * tk:(p + 1) * tk, g * gw:(g + 1) * gw].astype(F32)
            o_ref[bb, p * tk:(p + 1) * tk, g * gw:(g + 1) * gw] = (
                o * _silu(zg)).astype(o_ref.dtype)


def _stick_breaking(proj):
    bsz, s, _ = proj.shape
    w = SB_WIDTH
    n_chains = len(SB_CHAINS)
    rows = SB_N_DIAG * SB_PART_ROWS
    return pl.pallas_call(
        _sb_kernel,
        out_shape=jax.ShapeDtypeStruct((bsz, s, w), BF16),
        grid=(bsz // SB_BPS, s // SB_TQ),
        in_specs=[
            pl.BlockSpec((SB_BPS, SB_TQ, w), lambda b, i: (b, i, COL_QA)),
            pl.BlockSpec((SB_BPS, s, w), lambda b, i: (b, 0, COL_KA)),
            pl.BlockSpec((SB_BPS, s, w), lambda b, i: (b, 0, COL_VA)),
            pl.BlockSpec((SB_BPS, SB_TQ, w), lambda b, i: (b, i, COL_ZA)),
            pl.BlockSpec((rows, SB_TK), lambda b, i: (0, 0)),
        ],
        out_specs=pl.BlockSpec((SB_BPS, SB_TQ, w), lambda b, i: (b, i, 0)),
        scratch_shapes=[
            pltpu.VMEM((n_chains, rows, SB_GROUP_W), BF16),
            pltpu.VMEM((n_chains, rows, SB_GROUP_W), F32),
            pltpu.VMEM((n_chains, rows, 1), F32),
            pltpu.SMEM((n_chains,), F32),
            pltpu.SMEM((n_chains * SB_N_DIAG,), F32),
        ],
        compiler_params=pltpu.CompilerParams(
            dimension_semantics=("arbitrary", "arbitrary"),
            vmem_limit_bytes=V7X_VMEM_LIMIT_BYTES),
        name="stick_breaking",
    )(proj, proj, proj, proj, jnp.asarray(_sb_band_mask(), BF16))


def _hgrn_sum_matrix():
    c = HG_C
    w = np.zeros((HG_NMAT, c, c), np.float32)
    for i, lvl in enumerate(HG_MM_LEVELS):
        n = 1 << lvl
        for r in range(c):
            m = (r // n) * n + n // 2
            if r >= m:
                w[i, r, m:r + 1] = 1.0
            else:
                w[i, r, r + 1:m] = 1.0
    for r in range(c):
        w[HG_NMAT - 2, r, :r + 1] = 1.0
        w[HG_NMAT - 1, r, r + 1:] = 1.0
    return w.reshape(HG_NMAT * c, c)


def _hgrn_level_matrix():
    c = HG_C
    lv = np.full((c, c), -1, np.int32)
    for t in range(c):
        for s in range(t + 1):
            lv[t, s] = t - s if t - s <= HG_BAND else HG_BAND + (t ^ s).bit_length()
    return lv


def _hgrn_kernel(layer, q_ref, f_ref, i_ref, z_ref, lb_ref, nw_ref, w_ref, lv_ref, o_ref,
                 state_ref):
    c = HG_C
    dk = HG_HEAD_DIM
    ci = pl.program_id(1)

    @pl.when(ci == 0)
    def _():
        state_ref[...] = jnp.zeros_like(state_ref)

    lbw = lb_ref[...]
    e = jnp.exp(lbw - jnp.max(lbw, axis=0, keepdims=True))
    p = e / jnp.sum(e, axis=0, keepdims=True)
    lb_all = jnp.zeros((1, HG_WIDTH), F32)
    for l in range(1, layer + 1):
        lb_all = lb_all + p[l:l + 1]

    lv = lv_ref[...]
    for cc, h in [(cc, h) for cc in range(HG_CPS) for h in range(HG_HEADS)]:
        rws = slice(cc * c, (cc + 1) * c)
        cols = slice(h * dk, (h + 1) * dk)
        if h == 0:
            f_all = lb_all + (1.0 - lb_all) * _sigmoid(f_ref[0, rws, :].astype(F32))
            g_all = jnp.log(f_all) * LOG2E
            g_hi, g_lo = _split_bf16(g_all)
        if h % 2 == 0:
            pair = slice(h * dk, (h + 2) * dk)
            g_cat = jnp.concatenate([g_hi[:, pair], g_lo[:, pair]], axis=0)
            ex_pair = jnp.exp2(_dot(w_ref[...], g_cat))
            ex_pair_bf = ex_pair.astype(BF16)
        half = slice((h % 2) * dk, (h % 2 + 1) * dk)
        ex_bf = ex_pair_bf[:, half]
        f_h = f_all[:, cols]
        k_f = 1.0 - f_h
        q_f = _silu(q_ref[0, rws, cols].astype(F32))
        k_bf = k_f.astype(BF16)
        q_bf = q_f.astype(BF16)
        v_bf = i_ref[0, rws, cols]

        scores = jnp.where(lv == 0, jnp.sum(q_f * k_f, axis=-1, keepdims=True), 0.0)
        qd = q_f
        for j in range(1, HG_BAND + 1):
            qd = qd * (f_h if j == 1 else pltpu.roll(f_h, j - 1, axis=0))
            d_j = jnp.sum(qd * pltpu.roll(k_f, j, axis=0), axis=-1, keepdims=True)
            scores = jnp.where(lv == j, d_j, scores)
        for i, lvl in enumerate(HG_MM_LEVELS):
            dec = ex_bf[i * c:(i + 1) * c]
            scores = jnp.where(lv == HG_BAND + lvl, _nt_dot(q_bf * dec, k_bf * dec), scores)

        dec_q = ex_bf[(HG_NMAT - 2) * c:(HG_NMAT - 1) * c]
        dec_k = ex_bf[(HG_NMAT - 1) * c:HG_NMAT * c]
        state = state_ref[h]
        inter = _nt_dot(q_bf * dec_q, state.astype(BF16))
        intra = _dot(scores.astype(BF16), v_bf)
        o = inter + intra

        v_t = v_bf.astype(F32).T.astype(BF16)
        dec_end = ex_pair[(HG_NMAT - 1) * c - 1:(HG_NMAT - 1) * c, half]
        state_ref[h] = state * dec_end + _dot(v_t, k_bf * dec_k)

        ms = jnp.mean(o * o, axis=-1, keepdims=True)
        o = o * lax.rsqrt(ms + RMS_EPS) * nw_ref[...]
        o_ref[0, rws, cols] = (o * _silu(z_ref[0, rws, cols].astype(F32))).astype(o_ref.dtype)


def _hgrn2(proj, lower_bounds, norm_w_l, layer):
    bsz, s, _ = proj.shape
    depth = lower_bounds.shape[0]
    dk = HG_HEAD_DIM
    w = HG_WIDTH
    w_one = _hgrn_sum_matrix()
    w_sum = jnp.asarray(np.concatenate([w_one, w_one], axis=1), BF16)
    lv = jnp.asarray(_hgrn_level_matrix())
    tr = HG_CPS * HG_C
    return pl.pallas_call(
        partial(_hgrn_kernel, layer),
        out_shape=jax.ShapeDtypeStruct((bsz, s, w), BF16),
        grid=(bsz, s // tr),
        in_specs=[
            pl.BlockSpec((1, tr, w), lambda b, i: (b, i, COL_QB)),
            pl.BlockSpec((1, tr, w), lambda b, i: (b, i, COL_FB)),
            pl.BlockSpec((1, tr, w), lambda b, i: (b, i, COL_IB)),
            pl.BlockSpec((1, tr, w), lambda b, i: (b, i, COL_ZB)),
            pl.BlockSpec((depth, w), lambda b, i: (0, 0)),
            pl.BlockSpec((1, dk), lambda b, i: (0, 0)),
            pl.BlockSpec((HG_NMAT * HG_C, 2 * HG_C), lambda b, i: (0, 0)),
            pl.BlockSpec((HG_C, HG_C), lambda b, i: (0, 0)),
        ],
        out_specs=pl.BlockSpec((1, tr, w), lambda b, i: (b, i, 0)),
        scratch_shapes=[pltpu.VMEM((HG_HEADS, dk, dk), F32)],
        compiler_params=pltpu.CompilerParams(
            dimension_semantics=("arbitrary", "arbitrary"),
            vmem_limit_bytes=V7X_VMEM_LIMIT_BYTES),
        name="hgrn2",
    )(proj, proj, proj, proj, lower_bounds, norm_w_l.reshape(1, dk), w_sum, lv)


def _merge_kernel(alpha, emit_h, ya_ref, yb_ref, pre_ref, post_ref, u_ref, zc_ref,
                  pre_h_ref, u_h_ref, ga_ref, gb_ref, gc_ref, x_ref, gate_ref, cw_ref,
                  wb_ref, wo_ref, lng_ref, lnb_ref, *rest):
    if emit_h:
        nmod_ref, xo_ref, h_ref, ext_ref = rest
    else:
        xo_ref, ext_ref = rest
    i = pl.program_id(1)
    tm = pre_ref.shape[1]

    hp = pre_h_ref[0, HALO_ROWS - SUBLANES:HALO_ROWS, :].astype(F32)
    hu = u_h_ref[0, HALO_ROWS - SUBLANES:HALO_ROWS, :].astype(F32)
    ext_ref[0:SUBLANES, :] = jnp.where(i > 0, hp * hu, 0.0)
    ext_ref[SUBLANES:SUBLANES + tm, :] = pre_ref[0].astype(F32) * u_ref[0].astype(F32)
    cw = cw_ref[...]
    conv = (cw[2:3] * ext_ref[SUBLANES:SUBLANES + tm, :]
            + cw[1:2] * ext_ref[SUBLANES - 1:SUBLANES - 1 + tm, :]
            + cw[0:1] * ext_ref[SUBLANES - 2:SUBLANES - 2 + tm, :])
    y_c = (post_ref[0].astype(F32) * conv * _silu(zc_ref[0].astype(F32))).astype(BF16)

    def gate(ref):
        return 1.0 / (1.0 + jnp.exp2(ref[0].astype(F32)))

    merged = (gate(ga_ref) * _dot(ya_ref[0], wb_ref[0])
              + gate(gb_ref) * _dot(yb_ref[0], wb_ref[1])
              + gate(gc_ref) * _dot(y_c, wb_ref[2]))
    y = _dot(merged.astype(BF16), wo_ref[...])

    r = alpha * x_ref[0] + (1.0 + gate_ref[0]) * y
    x_new = _standardize(r) * lng_ref[...] + lnb_ref[...]
    xo_ref[0] = x_new
    if emit_h:
        d = x_new.shape[-1]
        shift = nmod_ref[0, :, 0:d]
        scale = nmod_ref[0, :, d:2 * d]
        h_ref[0] = (_standardize(x_new) * (1.0 + scale) + shift).astype(h_ref.dtype)


def _merge(ya, yb, proj, x, mod_l, conv_w_l, wb_l, wo_l, lng_l, lnb_l, alpha, next_mod):
    bsz, s, d = x.shape
    tm = MERGE_TM
    cw = CONV_WIDTH
    emit_h = next_mod is not None
    halo_blocks = tm // HALO_ROWS

    def halo_map(col):
        return lambda b, i: (b, jnp.maximum(i * halo_blocks - 1, 0), col)

    in_specs = [
        pl.BlockSpec((1, tm, SB_WIDTH), lambda b, i: (b, i, 0)),
        pl.BlockSpec((1, tm, HG_WIDTH), lambda b, i: (b, i, 0)),
        pl.BlockSpec((1, tm, cw), lambda b, i: (b, i, COL_PRE)),
        pl.BlockSpec((1, tm, cw), lambda b, i: (b, i, COL_POST)),
        pl.BlockSpec((1, tm, cw), lambda b, i: (b, i, COL_U)),
        pl.BlockSpec((1, tm, cw), lambda b, i: (b, i, COL_ZC)),
        pl.BlockSpec((1, HALO_ROWS, cw), halo_map(COL_PRE)),
        pl.BlockSpec((1, HALO_ROWS, cw), halo_map(COL_U)),
        pl.BlockSpec((1, tm, d), lambda b, i: (b, i, COL_GATES)),
        pl.BlockSpec((1, tm, d), lambda b, i: (b, i, COL_GATES + 1)),
        pl.BlockSpec((1, tm, d), lambda b, i: (b, i, COL_GATES + 2)),
        pl.BlockSpec((1, tm, d), lambda b, i: (b, i, 0)),
        pl.BlockSpec((1, 1, d), lambda b, i: (b, 0, 2)),
        pl.BlockSpec((CONV_K, cw), lambda b, i: (0, 0)),
        pl.BlockSpec((3, SB_WIDTH, d), lambda b, i: (0, 0, 0)),
        pl.BlockSpec((d, d), lambda b, i: (0, 0)),
        pl.BlockSpec((1, d), lambda b, i: (0, 0)),
        pl.BlockSpec((1, d), lambda b, i: (0, 0)),
    ]
    args = [ya, yb] + [proj] * 9 + [x, mod_l, conv_w_l, wb_l, wo_l,
            lng_l.reshape(1, d), lnb_l.reshape(1, d)]
    out_shape = [jax.ShapeDtypeStruct((bsz, s, d), F32)]
    out_specs = [pl.BlockSpec((1, tm, d), lambda b, i: (b, i, 0))]
    if emit_h:
        in_specs.append(pl.BlockSpec((1, 1, 3 * d), lambda b, i: (b, 0, 0)))
        args.append(next_mod)
        out_shape.append(jax.ShapeDtypeStruct((bsz, s, d), BF16))
        out_specs.append(pl.BlockSpec((1, tm, d), lambda b, i: (b, i, 0)))
    outs = pl.pallas_call(
        partial(_merge_kernel, alpha, emit_h),
        out_shape=out_shape,
        grid=(bsz, s // tm),
        in_specs=in_specs,
        out_specs=out_specs,
        scratch_shapes=[pltpu.VMEM((tm + SUBLANES, cw), F32)],
        compiler_params=pltpu.CompilerParams(
            dimension_semantics=("arbitrary", "arbitrary"),
            vmem_limit_bytes=V7X_VMEM_LIMIT_BYTES),
        name="merge_residual",
    )(*args)
    return (outs[0], outs[1]) if emit_h else (outs[0], None)


def _in_proj_col_scale():
    cs = np.ones((1, IN_COLS), np.float32)
    cs[:, COL_QA * COL_GROUP:(COL_QA + 1) * COL_GROUP] = LOG2E * SB_HEAD_DIM ** -0.5
    cs[:, COL_GATES * D_MODEL:] = -LOG2E
    return jnp.asarray(cs)


def kernel(x, c, w_mod, b_mod, w_in, conv_w, hgrn_norm_w, lower_bounds, w_branch, w_out, ln_g, ln_b):
    bsz, s, d = x.shape
    depth = w_mod.shape[0]
    alpha = (2.0 * depth) ** 0.25

    mod = _modulation(c, w_mod, b_mod).reshape(depth, bsz, 1, 3 * d)
    col_scale = _in_proj_col_scale()

    h = _lnmod(x, mod[0])
    for l in range(depth):
        proj = _in_proj(h.reshape(bsz * s, d), w_in, l, col_scale).reshape(bsz, s, IN_COLS)
        ya = _stick_breaking(proj)
        yb = _hgrn2(proj, lower_bounds, hgrn_norm_w[l], l)
        next_mod = mod[l + 1] if l + 1 < depth else None
        x, h = _merge(ya, yb, proj, x, mod[l], conv_w[l], w_branch[l].astype(BF16),
                      w_out[l].astype(BF16), ln_g[l], ln_b[l], alpha, next_mod)
    return x
```

```python
from functools import partial

import numpy as np
import jax
import jax.numpy as jnp
from jax import lax
from jax.experimental import pallas as pl
from jax.experimental.pallas import tpu as pltpu

D_MODEL = 1024
SB_HEADS = 8
SB_HEAD_DIM = 64
SB_WIDTH = SB_HEADS * SB_HEAD_DIM
HG_HEADS = 4
HG_HEAD_DIM = 128
HG_WIDTH = HG_HEADS * HG_HEAD_DIM
CONV_WIDTH = 512
CONV_K = 3
LN_EPS = 1e-5
RMS_EPS = 1e-6

F32 = jnp.float32
BF16 = jnp.bfloat16

V7X_VMEM_LIMIT_BYTES = 56 * 1024 * 1024
SUBLANES = 8
LANES = 128
HALO_ROWS = 16

COL_GROUP = SB_WIDTH
N_COL_GROUPS = 12
IN_COLS = N_COL_GROUPS * COL_GROUP + 3 * D_MODEL
(COL_QA, COL_KA, COL_VA, COL_ZA, COL_QB, COL_FB, COL_IB, COL_ZB,
 COL_PRE, COL_POST, COL_U, COL_ZC) = range(N_COL_GROUPS)
COL_GATES = N_COL_GROUPS * COL_GROUP // D_MODEL
IN_PROJ_TM = 2048
IN_PROJ_TN = 1536
MERGE_TM = 512
LNMOD_TM = 2048

SB_GROUP = 4
SB_GROUP_W = SB_GROUP * SB_HEAD_DIM
SB_N_GROUPS = SB_HEADS // SB_GROUP
SB_TQ = 512
SB_TK = 256
SB_N_DIAG = SB_TQ // SB_TK
SB_PART_ROWS = SB_GROUP * SB_TK
SB_BPS = 2
SB_UNDERFLOW_BITS = 160.0
SB_BOUND_SLACK = 1.05
SB_CHAINS = [(bb, g) for bb in range(SB_BPS) for g in range(SB_N_GROUPS)]
LOG2E = 1.4426950408889634
LOG2E_BF16_HI = 1.4453125
LOG2E_BF16_LO = -0.00262451171875

HG_C = 128
HG_CPS = 4
HG_LEVELS = 7
HG_BAND = 3
HG_MM_LEVELS = range(3, HG_LEVELS + 1)
HG_NMAT = len(HG_MM_LEVELS) + 2


def _nt_dot(a, b):
    return lax.dot_general(a, b, (((1,), (1,)), ((), ())), preferred_element_type=F32)


def _dot(a, b):
    return jnp.dot(a, b, preferred_element_type=F32)


def _split_bf16(v):
    hi = v.astype(BF16)
    lo = (v - hi.astype(F32)).astype(BF16)
    return hi, lo


def _sigmoid(v):
    return 1.0 / (1.0 + jnp.exp(-v))


def _silu(v):
    return v * _sigmoid(v)


def _standardize(xf):
    mu = jnp.mean(xf, axis=-1, keepdims=True)
    xc = xf - mu
    var = jnp.mean(xc * xc, axis=-1, keepdims=True)
    return xc * lax.rsqrt(var + LN_EPS)


def _mod_kernel(c_ref, w_ref, b_ref, o_ref):
    c_hi, c_lo = _split_bf16(c_ref[...])
    w_hi, w_lo = _split_bf16(w_ref[0])
    acc = _dot(c_hi, w_hi) + (_dot(c_hi, w_lo) + _dot(c_lo, w_hi))
    o_ref[0] = acc + b_ref[0]


def _modulation(c, w_mod, b_mod):
    depth, d, n = w_mod.shape
    bsz = c.shape[0]
    tn = 1024
    return pl.pallas_call(
        _mod_kernel,
        out_shape=jax.ShapeDtypeStruct((depth, bsz, n), F32),
        grid=(depth, n // tn),
        in_specs=[
            pl.BlockSpec((bsz, d), lambda l, j: (0, 0)),
            pl.BlockSpec((1, d, tn), lambda l, j: (l, 0, j)),
            pl.BlockSpec((1, 1, tn), lambda l, j: (l, 0, j)),
        ],
        out_specs=pl.BlockSpec((1, bsz, tn), lambda l, j: (l, 0, j)),
        compiler_params=pltpu.CompilerParams(
            dimension_semantics=("arbitrary", "arbitrary"),
            vmem_limit_bytes=V7X_VMEM_LIMIT_BYTES),
        name="adaln_mod",
    )(c, w_mod, b_mod.reshape(depth, 1, n))


def _lnmod_kernel(x_ref, mod_ref, h_ref):
    d = x_ref.shape[-1]
    shift = mod_ref[0, :, 0:d]
    scale = mod_ref[0, :, d:2 * d]
    h_ref[0] = (_standardize(x_ref[0]) * (1.0 + scale) + shift).astype(h_ref.dtype)


def _lnmod(x, mod_l):
    bsz, s, d = x.shape
    tm = LNMOD_TM
    return pl.pallas_call(
        _lnmod_kernel,
        out_shape=jax.ShapeDtypeStruct((bsz, s, d), BF16),
        grid=(bsz, s // tm),
        in_specs=[
            pl.BlockSpec((1, tm, d), lambda b, i: (b, i, 0)),
            pl.BlockSpec((1, 1, 3 * d), lambda b, i: (b, 0, 0)),
        ],
        out_specs=pl.BlockSpec((1, tm, d), lambda b, i: (b, i, 0)),
        compiler_params=pltpu.CompilerParams(
            dimension_semantics=("arbitrary", "arbitrary"),
            vmem_limit_bytes=V7X_VMEM_LIMIT_BYTES),
        name="ln_modulate",
    )(x, mod_l)


def _in_proj_kernel(h_ref, w_ref, cs_ref, o_ref, wbf_ref):
    @pl.when(pl.program_id(1) == 0)
    def _():
        wbf_ref[...] = (w_ref[0] * cs_ref[...]).astype(BF16)

    o_ref[...] = _dot(h_ref[...], wbf_ref[...]).astype(o_ref.dtype)


def _in_proj(h2, w_in, layer, col_scale):
    m, k = h2.shape
    n = w_in.shape[2]
    tm, tn = IN_PROJ_TM, IN_PROJ_TN
    return pl.pallas_call(
        _in_proj_kernel,
        out_shape=jax.ShapeDtypeStruct((m, n), BF16),
        grid=(n // tn, m // tm),
        in_specs=[
            pl.BlockSpec((tm, k), lambda j, i: (i, 0)),
            pl.BlockSpec((1, k, tn), lambda j, i: (layer, 0, j)),
            pl.BlockSpec((1, tn), lambda j, i: (0, j)),
        ],
        out_specs=pl.BlockSpec((tm, tn), lambda j, i: (i, j)),
        scratch_shapes=[pltpu.VMEM((k, tn), BF16)],
        compiler_params=pltpu.CompilerParams(
            dimension_semantics=("arbitrary", "arbitrary"),
            vmem_limit_bytes=V7X_VMEM_LIMIT_BYTES),
        name="in_proj",
    )(h2, w_in, col_scale)


def _sb_band_mask():
    t_in = np.arange(SB_PART_ROWS)[:, None] % SB_TK
    return (np.arange(SB_TK)[None, :] < t_in).astype(np.float32)


def _sb_kernel(q_ref, k_ref, v_ref, z_ref, dm_ref, o_ref, qs_ref, acc_ref, car_ref, kmax2_ref,
               cmin_ref):
    qi = pl.program_id(1)
    tk, gw = SB_TK, SB_GROUP_W
    rows = SB_N_DIAG * SB_PART_ROWS

    lane_head = lax.broadcasted_iota(jnp.int32, (tk, gw), 1) // SB_HEAD_DIM
    head_row = lax.broadcasted_iota(jnp.int32, (1, gw), 1) // SB_HEAD_DIM
    head_keep = [jnp.where(head_row == h, 1.0, 0.0).astype(BF16) for h in range(SB_GROUP)]
    for c, (bb, g) in enumerate(SB_CHAINS):
        for p in range(SB_N_DIAG):
            q_bf = q_ref[bb, p * tk:(p + 1) * tk, g * gw:(g + 1) * gw]
            qs_ref[c, p * SB_PART_ROWS:(p + 1) * SB_PART_ROWS, :] = jnp.concatenate(
                [q_bf * head_keep[h] for h in range(SB_GROUP)], axis=0)

    jj = lax.broadcasted_iota(jnp.int32, (tk, tk), 0)
    ss = lax.broadcasted_iota(jnp.int32, (tk, tk), 1)
    u_mat = jnp.where(jj >= ss, 1.0, 0.0).astype(BF16)

    def softplus2_bf16(z):
        zb = z.astype(BF16)
        ln_w = jnp.log(1.0 + jnp.exp2(-jnp.abs(zb)))
        return jnp.maximum(zb, 0.0) + (ln_w * LOG2E_BF16_HI + ln_w * LOG2E_BF16_LO)

    def block(kb, band, r_hi=rows):
        start = pl.multiple_of(kb * tk, tk)
        r0 = 0 if band is None else band * SB_PART_ROWS
        rws = slice(r0, r_hi)
        for c, (bb, g) in enumerate(SB_CHAINS):
            gcols = slice(g * gw, (g + 1) * gw)
            z = _nt_dot(qs_ref[c, rws, :], k_ref[bb, pl.ds(start, tk), gcols])
            sp = softplus2_bf16(z)
            if band is not None:
                r1 = r0 + SB_PART_ROWS
                tri = dm_ref[...]
                sp_top = sp[0:SB_PART_ROWS] * tri
                sp = sp_top if r1 == rows else jnp.concatenate([sp_top, sp[SB_PART_ROWS:]], axis=0)
            rb = _dot(sp, u_mat)
            if band is None:
                car = car_ref[c, rws, :]
            else:
                car = jnp.zeros((SB_PART_ROWS, 1), F32)
                if r1 < rows:
                    car = jnp.concatenate([car, car_ref[c, r1:rows, :]], axis=0)
            arg = z - (rb + car)
            if band is not None:
                a = jnp.exp2(jnp.minimum(arg[0:SB_PART_ROWS], 0.0).astype(BF16)) * tri
                if r1 < rows:
                    a = jnp.concatenate([a, jnp.exp2(arg[SB_PART_ROWS:].astype(BF16))], axis=0)
            else:
                a = jnp.exp2(arg.astype(BF16))
            av = _dot(a, v_ref[bb, pl.ds(start, tk), gcols])
            if band is None:
                acc_ref[c, rws, :] += av
            else:
                acc_ref[c, r0:r1, :] = av[0:SB_PART_ROWS]
                if r1 < rows:
                    acc_ref[c, r1:rows, :] += av[SB_PART_ROWS:]
            new_car = car + rb[:, 0:1]
            car_ref[c, rws, :] = new_car
            for p in range(r0 // SB_PART_ROWS, r_hi // SB_PART_ROWS):
                lo = p * SB_PART_ROWS - r0
                cmin_ref[c * SB_N_DIAG + p] = jnp.min(new_car[lo:lo + SB_PART_ROWS])

    first_kb = qi * SB_N_DIAG
    for d in reversed(range(SB_N_DIAG)):
        block(first_kb + d, d)

    @pl.when(qi == 0)
    def _():
        for c, (bb, g) in enumerate(SB_CHAINS):
            kf = k_ref[bb, :, g * gw:(g + 1) * gw].astype(F32)
            kmax2_ref[c] = jnp.max(jnp.sum(kf * kf, axis=-1, keepdims=True))

    seg_l = lax.broadcasted_iota(jnp.int32, (gw, LANES), 0) // SB_HEAD_DIM
    seg_j = lax.broadcasted_iota(jnp.int32, (gw, LANES), 1)
    seg = jnp.where(seg_l == seg_j, 1.0, 0.0).astype(BF16)
    finished_at = []
    for c, (bb, g) in enumerate(SB_CHAINS):
        qf = q_ref[bb, :, g * gw:(g + 1) * gw].astype(F32)
        qmax2 = jnp.max(_dot((qf * qf).astype(BF16), seg))
        zmax = jnp.sqrt(jnp.full((SUBLANES, LANES), qmax2 * kmax2_ref[c] * SB_BOUND_SLACK, F32))
        finished_at.append(jnp.max(zmax) + SB_UNDERFLOW_BITS)

    def flags():
        go_head = jnp.bool_(False)
        go_tail = jnp.bool_(False)
        for c in range(len(SB_CHAINS)):
            go_head = jnp.logical_or(go_head, cmin_ref[c * SB_N_DIAG] <= finished_at[c])
            for p in range(1, SB_N_DIAG):
                go_tail = jnp.logical_or(go_tail, cmin_ref[c * SB_N_DIAG + p] <= finished_at[c])
        return go_head, go_tail

    def cond(carry):
        i, go_head, go_tail = carry
        return jnp.logical_and(i < first_kb, jnp.logical_or(go_head, go_tail))

    def body(carry):
        i, _, go_tail = carry
        kb = first_kb - 1 - i

        @pl.when(go_tail)
        def _():
            block(kb, None)

        @pl.when(jnp.logical_not(go_tail))
        def _():
            block(kb, None, SB_PART_ROWS)

        return (i + 1,) + flags()

    lax.while_loop(cond, body, (jnp.int32(0),) + flags())

    for c, (bb, g) in enumerate(SB_CHAINS):
        for p in range(SB_N_DIAG):
            r = p * SB_PART_ROWS
            o = acc_ref[c, r:r + tk, :]
            for h in range(1, SB_GROUP):
                o = jnp.where(lane_head == h, acc_ref[c, r + h * tk:r + (h + 1) * tk, :], o)
            zg = z_ref[bb, p * tk:(p + 1) * tk, g * gw:(g + 1) * gw].astype(F32)
            o_ref[bb, p * tk:(p + 1) * tk, g * gw:(g + 1) * gw] = (
                o * _silu(zg)).astype(o_ref.dtype)


def _stick_breaking(proj):
    bsz, s, _ = proj.shape
    w = SB_WIDTH
    n_chains = len(SB_CHAINS)
    rows = SB_N_DIAG * SB_PART_ROWS
    return pl.pallas_call(
        _sb_kernel,
        out_shape=jax.ShapeDtypeStruct((bsz, s, w), BF16),
        grid=(bsz // SB_BPS, s // SB_TQ),
        in_specs=[
            pl.BlockSpec((SB_BPS, SB_TQ, w), lambda b, i: (b, i, COL_QA)),
            pl.BlockSpec((SB_BPS, s, w), lambda b, i: (b, 0, COL_KA)),
            pl.BlockSpec((SB_BPS, s, w), lambda b, i: (b, 0, COL_VA)),
            pl.BlockSpec((SB_BPS, SB_TQ, w), lambda b, i: (b, i, COL_ZA)),
            pl.BlockSpec((SB_PART_ROWS, SB_TK), lambda b, i: (0, 0)),
        ],
        out_specs=pl.BlockSpec((SB_BPS, SB_TQ, w), lambda b, i: (b, i, 0)),
        scratch_shapes=[
            pltpu.VMEM((n_chains, rows, SB_GROUP_W), BF16),
            pltpu.VMEM((n_chains, rows, SB_GROUP_W), F32),
            pltpu.VMEM((n_chains, rows, 1), F32),
            pltpu.SMEM((n_chains,), F32),
            pltpu.SMEM((n_chains * SB_N_DIAG,), F32),
        ],
        compiler_params=pltpu.CompilerParams(
            dimension_semantics=("arbitrary", "arbitrary"),
            vmem_limit_bytes=V7X_VMEM_LIMIT_BYTES),
        name="stick_breaking",
    )(proj, proj, proj, proj, jnp.asarray(_sb_band_mask(), BF16))


def _hgrn_sum_matrix():
    c = HG_C
    w = np.zeros((HG_NMAT, c, c), np.float32)
    for i, lvl in enumerate(HG_MM_LEVELS):
        n = 1 << lvl
        for r in range(c):
            m = (r // n) * n + n // 2
            if r >= m:
                w[i, r, m:r + 1] = 1.0
            else:
                w[i, r, r + 1:m] = 1.0
    for r in range(c):
        w[HG_NMAT - 2, r, :r + 1] = 1.0
        w[HG_NMAT - 1, r, r + 1:] = 1.0
    return w.reshape(HG_NMAT * c, c)


def _hgrn_level_matrix():
    c = HG_C
    lv = np.full((c, c), -1, np.int32)
    for t in range(c):
        for s in range(t + 1):
            lv[t, s] = t - s if t - s <= HG_BAND else HG_BAND + (t ^ s).bit_length()
    return lv


def _hgrn_kernel(layer, q_ref, f_ref, i_ref, z_ref, lb_ref, nw_ref, w_ref, lv_ref, o_ref,
                 state_ref):
    c = HG_C
    dk = HG_HEAD_DIM
    ci = pl.program_id(1)

    @pl.when(ci == 0)
    def _():
        state_ref[...] = jnp.zeros_like(state_ref)

    lbw = lb_ref[...]
    e = jnp.exp(lbw - jnp.max(lbw, axis=0, keepdims=True))
    p = e / jnp.sum(e, axis=0, keepdims=True)
    lb_all = jnp.zeros((1, HG_WIDTH), F32)
    for l in range(1, layer + 1):
        lb_all = lb_all + p[l:l + 1]

    lv = lv_ref[...]
    for cc, h in [(cc, h) for cc in range(HG_CPS) for h in range(HG_HEADS)]:
        rws = slice(cc * c, (cc + 1) * c)
        cols = slice(h * dk, (h + 1) * dk)
        if h == 0:
            f_all = lb_all + (1.0 - lb_all) * _sigmoid(f_ref[0, rws, :].astype(F32))
            g_all = jnp.log(f_all) * LOG2E
            g_hi, g_lo = _split_bf16(g_all)
        if h % 2 == 0:
            pair = slice(h * dk, (h + 2) * dk)
            g_cat = jnp.concatenate([g_hi[:, pair], g_lo[:, pair]], axis=0)
            ex_pair = jnp.exp2(_dot(w_ref[...], g_cat))
            ex_pair_bf = ex_pair.astype(BF16)
        half = slice((h % 2) * dk, (h % 2 + 1) * dk)
        ex_bf = ex_pair_bf[:, half]
        f_h = f_all[:, cols]
        k_f = 1.0 - f_h
        q_f = _silu(q_ref[0, rws, cols].astype(F32))
        k_bf = k_f.astype(BF16)
        q_bf = q_f.astype(BF16)
        v_bf = i_ref[0, rws, cols]

        scores = jnp.where(lv == 0, jnp.sum(q_f * k_f, axis=-1, keepdims=True), 0.0)
        qd = q_f
        for j in range(1, HG_BAND + 1):
            qd = qd * (f_h if j == 1 else pltpu.roll(f_h, j - 1, axis=0))
            d_j = jnp.sum(qd * pltpu.roll(k_f, j, axis=0), axis=-1, keepdims=True)
            scores = jnp.where(lv == j, d_j, scores)
        for i, lvl in enumerate(HG_MM_LEVELS):
            dec = ex_bf[i * c:(i + 1) * c]
            scores = jnp.where(lv == HG_BAND + lvl, _nt_dot(q_bf * dec, k_bf * dec), scores)

        dec_q = ex_bf[(HG_NMAT - 2) * c:(HG_NMAT - 1) * c]
        dec_k = ex_bf[(HG_NMAT - 1) * c:HG_NMAT * c]
        state = state_ref[h]
        inter = _nt_dot(q_bf * dec_q, state.astype(BF16))
        intra = _dot(scores.astype(BF16), v_bf)
        o = inter + intra

        v_t = v_bf.astype(F32).T.astype(BF16)
        dec_end = ex_pair[(HG_NMAT - 1) * c - 1:(HG_NMAT - 1) * c, half]
        state_ref[h] = state * dec_end + _dot(v_t, k_bf * dec_k)

        ms = jnp.mean(o * o, axis=-1, keepdims=True)
        o = o * lax.rsqrt(ms + RMS_EPS) * nw_ref[...]
        o_ref[0, rws, cols] = (o * _silu(z_ref[0, rws, cols].astype(F32))).astype(o_ref.dtype)


def _hgrn2(proj, lower_bounds, norm_w_l, layer):
    bsz, s, _ = proj.shape
    depth = lower_bounds.shape[0]
    dk = HG_HEAD_DIM
    w = HG_WIDTH
    w_one = _hgrn_sum_matrix()
    w_sum = jnp.asarray(np.concatenate([w_one, w_one], axis=1), BF16)
    lv = jnp.asarray(_hgrn_level_matrix())
    tr = HG_CPS * HG_C
    return pl.pallas_call(
        partial(_hgrn_kernel, layer),
        out_shape=jax.ShapeDtypeStruct((bsz, s, w), BF16),
        grid=(bsz, s // tr),
        in_specs=[
            pl.BlockSpec((1, tr, w), lambda b, i: (b, i, COL_QB)),
            pl.BlockSpec((1, tr, w), lambda b, i: (b, i, COL_FB)),
            pl.BlockSpec((1, tr, w), lambda b, i: (b, i, COL_IB)),
            pl.BlockSpec((1, tr, w), lambda b, i: (b, i, COL_ZB)),
            pl.BlockSpec((depth, w), lambda b, i: (0, 0)),
            pl.BlockSpec((1, dk), lambda b, i: (0, 0)),
            pl.BlockSpec((HG_NMAT * HG_C, 2 * HG_C), lambda b, i: (0, 0)),
            pl.BlockSpec((HG_C, HG_C), lambda b, i: (0, 0)),
        ],
        out_specs=pl.BlockSpec((1, tr, w), lambda b, i: (b, i, 0)),
        scratch_shapes=[pltpu.VMEM((HG_HEADS, dk, dk), F32)],
        compiler_params=pltpu.CompilerParams(
            dimension_semantics=("arbitrary", "arbitrary"),
            vmem_limit_bytes=V7X_VMEM_LIMIT_BYTES),
        name="hgrn2",
    )(proj, proj, proj, proj, lower_bounds, norm_w_l.reshape(1, dk), w_sum, lv)


def _merge_kernel(alpha, emit_h, ya_ref, yb_ref, pre_ref, post_ref, u_ref, zc_ref,
                  pre_h_ref, u_h_ref, ga_ref, gb_ref, gc_ref, x_ref, gate_ref, cw_ref,
                  wb_ref, wo_ref, lng_ref, lnb_ref, *rest):
    if emit_h:
        nmod_ref, xo_ref, h_ref, ext_ref = rest
    else:
        xo_ref, ext_ref = rest
    i = pl.program_id(1)
    tm = pre_ref.shape[1]

    hp = pre_h_ref[0, HALO_ROWS - SUBLANES:HALO_ROWS, :].astype(F32)
    hu = u_h_ref[0, HALO_ROWS - SUBLANES:HALO_ROWS, :].astype(F32)
    ext_ref[0:SUBLANES, :] = jnp.where(i > 0, hp * hu, 0.0)
    ext_ref[SUBLANES:SUBLANES + tm, :] = pre_ref[0].astype(F32) * u_ref[0].astype(F32)
    cw = cw_ref[...]
    conv = (cw[2:3] * ext_ref[SUBLANES:SUBLANES + tm, :]
            + cw[1:2] * ext_ref[SUBLANES - 1:SUBLANES - 1 + tm, :]
            + cw[0:1] * ext_ref[SUBLANES - 2:SUBLANES - 2 + tm, :])
    y_c = (post_ref[0].astype(F32) * conv * _silu(zc_ref[0].astype(F32))).astype(BF16)

    def gate(ref):
        return 1.0 / (1.0 + jnp.exp2(ref[0].astype(F32)))

    merged = (gate(ga_ref) * _dot(ya_ref[0], wb_ref[0])
              + gate(gb_ref) * _dot(yb_ref[0], wb_ref[1])
              + gate(gc_ref) * _dot(y_c, wb_ref[2]))
    y = _dot(merged.astype(BF16), wo_ref[...])

    r = alpha * x_ref[0] + (1.0 + gate_ref[0]) * y
    x_new = _standardize(r) * lng_ref[...] + lnb_ref[...]
    xo_ref[0] = x_new
    if emit_h:
        d = x_new.shape[-1]
        shift = nmod_ref[0, :, 0:d]
        scale = nmod_ref[0, :, d:2 * d]
        h_ref[0] = (_standardize(x_new) * (1.0 + scale) + shift).astype(h_ref.dtype)


def _merge(ya, yb, proj, x, mod_l, conv_w_l, wb_l, wo_l, lng_l, lnb_l, alpha, next_mod):
    bsz, s, d = x.shape
    tm = MERGE_TM
    cw = CONV_WIDTH
    emit_h = next_mod is not None
    halo_blocks = tm // HALO_ROWS

    def halo_map(col):
        return lambda b, i: (b, jnp.maximum(i * halo_blocks - 1, 0), col)

    in_specs = [
        pl.BlockSpec((1, tm, SB_WIDTH), lambda b, i: (b, i, 0)),
        pl.BlockSpec((1, tm, HG_WIDTH), lambda b, i: (b, i, 0)),
        pl.BlockSpec((1, tm, cw), lambda b, i: (b, i, COL_PRE)),
        pl.BlockSpec((1, tm, cw), lambda b, i: (b, i, COL_POST)),
        pl.BlockSpec((1, tm, cw), lambda b, i: (b, i, COL_U)),
        pl.BlockSpec((1, tm, cw), lambda b, i: (b, i, COL_ZC)),
        pl.BlockSpec((1, HALO_ROWS, cw), halo_map(COL_PRE)),
        pl.BlockSpec((1, HALO_ROWS, cw), halo_map(COL_U)),
        pl.BlockSpec((1, tm, d), lambda b, i: (b, i, COL_GATES)),
        pl.BlockSpec((1, tm, d), lambda b, i: (b, i, COL_GATES + 1)),
        pl.BlockSpec((1, tm, d), lambda b, i: (b, i, COL_GATES + 2)),
        pl.BlockSpec((1, tm, d), lambda b, i: (b, i, 0)),
        pl.BlockSpec((1, 1, d), lambda b, i: (b, 0, 2)),
        pl.BlockSpec((CONV_K, cw), lambda b, i: (0, 0)),
        pl.BlockSpec((3, SB_WIDTH, d), lambda b, i: (0, 0, 0)),
        pl.BlockSpec((d, d), lambda b, i: (0, 0)),
        pl.BlockSpec((1, d), lambda b, i: (0, 0)),
        pl.BlockSpec((1, d), lambda b, i: (0, 0)),
    ]
    args = [ya, yb] + [proj] * 9 + [x, mod_l, conv_w_l, wb_l, wo_l,
            lng_l.reshape(1, d), lnb_l.reshape(1, d)]
    out_shape = [jax.ShapeDtypeStruct((bsz, s, d), F32)]
    out_specs = [pl.BlockSpec((1, tm, d), lambda b, i: (b, i, 0))]
    if emit_h:
        in_specs.append(pl.BlockSpec((1, 1, 3 * d), lambda b, i: (b, 0, 0)))
        args.append(next_mod)
        out_shape.append(jax.ShapeDtypeStruct((bsz, s, d), BF16))
        out_specs.append(pl.BlockSpec((1, tm, d), lambda b, i: (b, i, 0)))
    outs = pl.pallas_call(
        partial(_merge_kernel, alpha, emit_h),
        out_shape=out_shape,
        grid=(bsz, s // tm),
        in_specs=in_specs,
        out_specs=out_specs,
        scratch_shapes=[pltpu.VMEM((tm + SUBLANES, cw), F32)],
        compiler_params=pltpu.CompilerParams(
            dimension_semantics=("arbitrary", "arbitrary"),
            vmem_limit_bytes=V7X_VMEM_LIMIT_BYTES),
        name="merge_residual",
    )(*args)
    return (outs[0], outs[1]) if emit_h else (outs[0], None)


def _in_proj_col_scale():
    cs = np.ones((1, IN_COLS), np.float32)
    cs[:, COL_QA * COL_GROUP:(COL_QA + 1) * COL_GROUP] = LOG2E * SB_HEAD_DIM ** -0.5
    cs[:, COL_GATES * D_MODEL:] = -LOG2E
    return jnp.asarray(cs)


def kernel(x, c, w_mod, b_mod, w_in, conv_w, hgrn_norm_w, lower_bounds, w_branch, w_out, ln_g, ln_b):
    bsz, s, d = x.shape
    depth = w_mod.shape[0]
    alpha = (2.0 * depth) ** 0.25

    mod = _modulation(c, w_mod, b_mod).reshape(depth, bsz, 1, 3 * d)
    col_scale = _in_proj_col_scale()

    h = _lnmod(x, mod[0])
    for l in range(depth):
        proj = _in_proj(h.reshape(bsz * s, d), w_in, l, col_scale).reshape(bsz, s, IN_COLS)
        ya = _stick_breaking(proj)
        yb = _hgrn2(proj, lower_bounds, hgrn_norm_w[l], l)
        next_mod = mod[l + 1] if l + 1 < depth else None
        x, h = _merge(ya, yb, proj, x, mod[l], conv_w[l], w_branch[l].astype(BF16),
                      w_out[l].astype(BF16), ln_g[l], ln_b[l], alpha, next_mod)
    return x
```

```python
from functools import partial

import numpy as np
import jax
import jax.numpy as jnp
from jax import lax
from jax.experimental import pallas as pl
from jax.experimental.pallas import tpu as pltpu

D_MODEL = 1024
SB_HEADS = 8
SB_HEAD_DIM = 64
SB_WIDTH = SB_HEADS * SB_HEAD_DIM
HG_HEADS = 4
HG_HEAD_DIM = 128
HG_WIDTH = HG_HEADS * HG_HEAD_DIM
CONV_WIDTH = 512
CONV_K = 3
LN_EPS = 1e-5
RMS_EPS = 1e-6

F32 = jnp.float32
BF16 = jnp.bfloat16

V7X_VMEM_LIMIT_BYTES = 56 * 1024 * 1024
SUBLANES = 8
LANES = 128
HALO_ROWS = 16

COL_GROUP = SB_WIDTH
N_COL_GROUPS = 12
IN_COLS = N_COL_GROUPS * COL_GROUP + 3 * D_MODEL
(COL_QA, COL_KA, COL_VA, COL_ZA, COL_QB, COL_FB, COL_IB, COL_ZB,
 COL_PRE, COL_POST, COL_U, COL_ZC) = range(N_COL_GROUPS)
COL_GATES = N_COL_GROUPS * COL_GROUP // D_MODEL
IN_PROJ_TM = 2048
IN_PROJ_TN = 1536
MERGE_TM = 512
LNMOD_TM = 2048

SB_GROUP = 4
SB_GROUP_W = SB_GROUP * SB_HEAD_DIM
SB_N_GROUPS = SB_HEADS // SB_GROUP
SB_TQ = 512
SB_TK = 256
SB_N_DIAG = SB_TQ // SB_TK
SB_PART_ROWS = SB_GROUP * SB_TK
SB_BPS = 2
SB_UNDERFLOW_BITS = 160.0
SB_BOUND_SLACK = 1.05
SB_CHAINS = [(bb, g) for bb in range(SB_BPS) for g in range(SB_N_GROUPS)]
LOG2E = 1.4426950408889634
LOG2E_BF16_HI = 1.4453125
LOG2E_BF16_LO = -0.00262451171875

HG_C = 128
HG_CPS = 8
HG_LEVELS = 7
HG_BAND = 3
HG_MM_LEVELS = range(3, HG_LEVELS + 1)
HG_NMAT = len(HG_MM_LEVELS) + 2


def _nt_dot(a, b):
    return lax.dot_general(a, b, (((1,), (1,)), ((), ())), preferred_element_type=F32)


def _dot(a, b):
    return jnp.dot(a, b, preferred_element_type=F32)


def _split_bf16(v):
    hi = v.astype(BF16)
    lo = (v - hi.astype(F32)).astype(BF16)
    return hi, lo


def _sigmoid(v):
    return 1.0 / (1.0 + jnp.exp(-v))


def _silu(v):
    return v * _sigmoid(v)


def _standardize(xf):
    mu = jnp.mean(xf, axis=-1, keepdims=True)
    xc = xf - mu
    var = jnp.mean(xc * xc, axis=-1, keepdims=True)
    return xc * lax.rsqrt(var + LN_EPS)


def _mod_kernel(c_ref, w_ref, b_ref, o_ref):
    c_hi, c_lo = _split_bf16(c_ref[...])
    w_hi, w_lo = _split_bf16(w_ref[0])
    acc = _dot(c_hi, w_hi) + (_dot(c_hi, w_lo) + _dot(c_lo, w_hi))
    o_ref[0] = acc + b_ref[0]


def _modulation(c, w_mod, b_mod):
    depth, d, n = w_mod.shape
    bsz = c.shape[0]
    tn = 1024
    return pl.pallas_call(
        _mod_kernel,
        out_shape=jax.ShapeDtypeStruct((depth, bsz, n), F32),
        grid=(depth, n // tn),
        in_specs=[
            pl.BlockSpec((bsz, d), lambda l, j: (0, 0)),
            pl.BlockSpec((1, d, tn), lambda l, j: (l, 0, j)),
            pl.BlockSpec((1, 1, tn), lambda l, j: (l, 0, j)),
        ],
        out_specs=pl.BlockSpec((1, bsz, tn), lambda l, j: (l, 0, j)),
        compiler_params=pltpu.CompilerParams(
            dimension_semantics=("arbitrary", "arbitrary"),
            vmem_limit_bytes=V7X_VMEM_LIMIT_BYTES),
        name="adaln_mod",
    )(c, w_mod, b_mod.reshape(depth, 1, n))


def _lnmod_kernel(x_ref, mod_ref, h_ref):
    d = x_ref.shape[-1]
    shift = mod_ref[0, :, 0:d]
    scale = mod_ref[0, :, d:2 * d]
    h_ref[0] = (_standardize(x_ref[0]) * (1.0 + scale) + shift).astype(h_ref.dtype)


def _lnmod(x, mod_l):
    bsz, s, d = x.shape
    tm = LNMOD_TM
    return pl.pallas_call(
        _lnmod_kernel,
        out_shape=jax.ShapeDtypeStruct((bsz, s, d), BF16),
        grid=(bsz, s // tm),
        in_specs=[
            pl.BlockSpec((1, tm, d), lambda b, i: (b, i, 0)),
            pl.BlockSpec((1, 1, 3 * d), lambda b, i: (b, 0, 0)),
        ],
        out_specs=pl.BlockSpec((1, tm, d), lambda b, i: (b, i, 0)),
        compiler_params=pltpu.CompilerParams(
            dimension_semantics=("arbitrary", "arbitrary"),
            vmem_limit_bytes=V7X_VMEM_LIMIT_BYTES),
        name="ln_modulate",
    )(x, mod_l)


def _in_proj_kernel(h_ref, w_ref, cs_ref, o_ref, wbf_ref):
    @pl.when(pl.program_id(1) == 0)
    def _():
        wbf_ref[...] = (w_ref[0] * cs_ref[...]).astype(BF16)

    o_ref[...] = _dot(h_ref[...], wbf_ref[...]).astype(o_ref.dtype)


def _in_proj(h2, w_in, layer, col_scale):
    m, k = h2.shape
    n = w_in.shape[2]
    tm, tn = IN_PROJ_TM, IN_PROJ_TN
    return pl.pallas_call(
        _in_proj_kernel,
        out_shape=jax.ShapeDtypeStruct((m, n), BF16),
        grid=(n // tn, m // tm),
        in_specs=[
            pl.BlockSpec((tm, k), lambda j, i: (i, 0)),
            pl.BlockSpec((1, k, tn), lambda j, i: (layer, 0, j)),
            pl.BlockSpec((1, tn), lambda j, i: (0, j)),
        ],
        out_specs=pl.BlockSpec((tm, tn), lambda j, i: (i, j)),
        scratch_shapes=[pltpu.VMEM((k, tn), BF16)],
        compiler_params=pltpu.CompilerParams(
            dimension_semantics=("arbitrary", "arbitrary"),
            vmem_limit_bytes=V7X_VMEM_LIMIT_BYTES),
        name="in_proj",
    )(h2, w_in, col_scale)


def _sb_band_mask():
    t_in = np.arange(SB_PART_ROWS)[:, None] % SB_TK
    return (np.arange(SB_TK)[None, :] < t_in).astype(np.float32)


def _sb_kernel(q_ref, k_ref, v_ref, z_ref, dm_ref, o_ref, qs_ref, acc_ref, car_ref, kmax2_ref,
               cmin_ref):
    qi = pl.program_id(1)
    tk, gw = SB_TK, SB_GROUP_W
    rows = SB_N_DIAG * SB_PART_ROWS

    lane_head = lax.broadcasted_iota(jnp.int32, (tk, gw), 1) // SB_HEAD_DIM
    head_row = lax.broadcasted_iota(jnp.int32, (1, gw), 1) // SB_HEAD_DIM
    head_keep = [jnp.where(head_row == h, 1.0, 0.0).astype(BF16) for h in range(SB_GROUP)]
    for c, (bb, g) in enumerate(SB_CHAINS):
        for p in range(SB_N_DIAG):
            q_bf = q_ref[bb, p * tk:(p + 1) * tk, g * gw:(g + 1) * gw]
            qs_ref[c, p * SB_PART_ROWS:(p + 1) * SB_PART_ROWS, :] = jnp.concatenate(
                [q_bf * head_keep[h] for h in range(SB_GROUP)], axis=0)

    jj = lax.broadcasted_iota(jnp.int32, (tk, tk), 0)
    ss = lax.broadcasted_iota(jnp.int32, (tk, tk), 1)
    u_mat = jnp.where(jj >= ss, 1.0, 0.0).astype(BF16)

    def softplus2_bf16(z):
        zb = z.astype(BF16)
        ln_w = jnp.log(1.0 + jnp.exp2(-jnp.abs(zb)))
        return jnp.maximum(zb, 0.0) + (ln_w * LOG2E_BF16_HI + ln_w * LOG2E_BF16_LO)

    def block(kb, band, r_hi=rows):
        start = pl.multiple_of(kb * tk, tk)
        r0 = 0 if band is None else band * SB_PART_ROWS
        rws = slice(r0, r_hi)
        for c, (bb, g) in enumerate(SB_CHAINS):
            gcols = slice(g * gw, (g + 1) * gw)
            z = _nt_dot(qs_ref[c, rws, :], k_ref[bb, pl.ds(start, tk), gcols])
            sp = softplus2_bf16(z)
            if band is not None:
                r1 = r0 + SB_PART_ROWS
                tri = dm_ref[...]
                sp_top = sp[0:SB_PART_ROWS] * tri
                sp = sp_top if r1 == rows else jnp.concatenate([sp_top, sp[SB_PART_ROWS:]], axis=0)
            rb = _dot(sp, u_mat)
            if band is None:
                car = car_ref[c, rws, :]
            else:
                car = jnp.zeros((SB_PART_ROWS, 1), F32)
                if r1 < rows:
                    car = jnp.concatenate([car, car_ref[c, r1:rows, :]], axis=0)
            arg = z - (rb + car)
            if band is not None:
                a = jnp.exp2(jnp.minimum(arg[0:SB_PART_ROWS], 0.0).astype(BF16)) * tri
                if r1 < rows:
                    a = jnp.concatenate([a, jnp.exp2(arg[SB_PART_ROWS:].astype(BF16))], axis=0)
            else:
                a = jnp.exp2(arg.astype(BF16))
            av = _dot(a, v_ref[bb, pl.ds(start, tk), gcols])
            if band is None:
                acc_ref[c, rws, :] += av
            else:
                acc_ref[c, r0:r1, :] = av[0:SB_PART_ROWS]
                if r1 < rows:
                    acc_ref[c, r1:rows, :] += av[SB_PART_ROWS:]
            new_car = car + rb[:, 0:1]
            car_ref[c, rws, :] = new_car
            for p in range(r0 // SB_PART_ROWS, r_hi // SB_PART_ROWS):
                lo = p * SB_PART_ROWS - r0
                cmin_ref[c * SB_N_DIAG + p] = jnp.min(new_car[lo:lo + SB_PART_ROWS])

    first_kb = qi * SB_N_DIAG
    for d in reversed(range(SB_N_DIAG)):
        block(first_kb + d, d)

    @pl.when(qi == 0)
    def _():
        for c, (bb, g) in enumerate(SB_CHAINS):
            kf = k_ref[bb, :, g * gw:(g + 1) * gw].astype(F32)
            kmax2_ref[c] = jnp.max(jnp.sum(kf * kf, axis=-1, keepdims=True))

    seg_l = lax.broadcasted_iota(jnp.int32, (gw, LANES), 0) // SB_HEAD_DIM
    seg_j = lax.broadcasted_iota(jnp.int32, (gw, LANES), 1)
    seg = jnp.where(seg_l == seg_j, 1.0, 0.0).astype(BF16)
    finished_at = []
    for c, (bb, g) in enumerate(SB_CHAINS):
        qf = q_ref[bb, :, g * gw:(g + 1) * gw].astype(F32)
        qmax2 = jnp.max(_dot((qf * qf).astype(BF16), seg))
        zmax = jnp.sqrt(jnp.full((SUBLANES, LANES), qmax2 * kmax2_ref[c] * SB_BOUND_SLACK, F32))
        finished_at.append(jnp.max(zmax) + SB_UNDERFLOW_BITS)

    def flags():
        go_head = jnp.bool_(False)
        go_tail = jnp.bool_(False)
        for c in range(len(SB_CHAINS)):
            go_head = jnp.logical_or(go_head, cmin_ref[c * SB_N_DIAG] <= finished_at[c])
            for p in range(1, SB_N_DIAG):
                go_tail = jnp.logical_or(go_tail, cmin_ref[c * SB_N_DIAG + p] <= finished_at[c])
        return go_head, go_tail

    def cond(carry):
        i, go_head, go_tail = carry
        return jnp.logical_and(i < first_kb, jnp.logical_or(go_head, go_tail))

    def body(carry):
        i, _, go_tail = carry
        kb = first_kb - 1 - i

        @pl.when(go_tail)
        def _():
            block(kb, None)

        @pl.when(jnp.logical_not(go_tail))
        def _():
            block(kb, None, SB_PART_ROWS)

        return (i + 1,) + flags()

    lax.while_loop(cond, body, (jnp.int32(0),) + flags())

    for c, (bb, g) in enumerate(SB_CHAINS):
        for p in range(SB_N_DIAG):
            r = p * SB_PART_ROWS
            o = acc_ref[c, r:r + tk, :]
            for h in range(1, SB_GROUP):
                o = jnp.where(lane_head == h, acc_ref[c, r + h * tk:r + (h + 1) * tk, :], o)
            zg = z_ref[bb, p * tk:(p + 1) * tk, g * gw:(g + 1) * gw].astype(F32)
            o_ref[bb, p * tk:(p + 1) * tk, g * gw:(g + 1) * gw] = (
                o * _silu(zg)).astype(o_ref.dtype)


def _stick_breaking(proj):
    bsz, s, _ = proj.shape
    w = SB_WIDTH
    n_chains = len(SB_CHAINS)
    rows = SB_N_DIAG * SB_PART_ROWS
    return pl.pallas_call(
        _sb_kernel,
        out_shape=jax.ShapeDtypeStruct((bsz, s, w), BF16),
        grid=(bsz // SB_BPS, s // SB_TQ),
        in_specs=[
            pl.BlockSpec((SB_BPS, SB_TQ, w), lambda b, i: (b, i, COL_QA)),
            pl.BlockSpec((SB_BPS, s, w), lambda b, i: (b, 0, COL_KA)),
            pl.BlockSpec((SB_BPS, s, w), lambda b, i: (b, 0, COL_VA)),
            pl.BlockSpec((SB_BPS, SB_TQ, w), lambda b, i: (b, i, COL_ZA)),
            pl.BlockSpec((SB_PART_ROWS, SB_TK), lambda b, i: (0, 0)),
        ],
        out_specs=pl.BlockSpec((SB_BPS, SB_TQ, w), lambda b, i: (b, i, 0)),
        scratch_shapes=[
            pltpu.VMEM((n_chains, rows, SB_GROUP_W), BF16),
            pltpu.VMEM((n_chains, rows, SB_GROUP_W), F32),
            pltpu.VMEM((n_chains, rows, 1), F32),
            pltpu.SMEM((n_chains,), F32),
            pltpu.SMEM((n_chains * SB_N_DIAG,), F32),
        ],
        compiler_params=pltpu.CompilerParams(
            dimension_semantics=("arbitrary", "arbitrary"),
            vmem_limit_bytes=V7X_VMEM_LIMIT_BYTES),
        name="stick_breaking",
    )(proj, proj, proj, proj, jnp.asarray(_sb_band_mask(), BF16))


def _hgrn_sum_matrix():
    c = HG_C
    w = np.zeros((HG_NMAT, c, c), np.float32)
    for i, lvl in enumerate(HG_MM_LEVELS):
        n = 1 << lvl
        for r in range(c):
            m = (r // n) * n + n // 2
            if r >= m:
                w[i, r, m:r + 1] = 1.0
            else:
                w[i, r, r + 1:m] = 1.0
    for r in range(c):
        w[HG_NMAT - 2, r, :r + 1] = 1.0
        w[HG_NMAT - 1, r, r + 1:] = 1.0
    return w.reshape(HG_NMAT * c, c)


def _hgrn_level_matrix():
    c = HG_C
    lv = np.full((c, c), -1, np.int32)
    for t in range(c):
        for s in range(t + 1):
            lv[t, s] = t - s if t - s <= HG_BAND else HG_BAND + (t ^ s).bit_length()
    return lv


def _hgrn_kernel(layer, q_ref, f_ref, i_ref, z_ref, lb_ref, nw_ref, w_ref, lv_ref, o_ref,
                 state_ref):
    c = HG_C
    dk = HG_HEAD_DIM
    ci = pl.program_id(1)

    @pl.when(ci == 0)
    def _():
        state_ref[...] = jnp.zeros_like(state_ref)

    lbw = lb_ref[...]
    e = jnp.exp(lbw - jnp.max(lbw, axis=0, keepdims=True))
    p = e / jnp.sum(e, axis=0, keepdims=True)
    lb_all = jnp.zeros((1, HG_WIDTH), F32)
    for l in range(1, layer + 1):
        lb_all = lb_all + p[l:l + 1]

    lv = lv_ref[...]
    for cc, h in [(cc, h) for cc in range(HG_CPS) for h in range(HG_HEADS)]:
        rws = slice(cc * c, (cc + 1) * c)
        cols = slice(h * dk, (h + 1) * dk)
        if h == 0:
            f_all = lb_all + (1.0 - lb_all) * _sigmoid(f_ref[0, rws, :].astype(F32))
            g_all = jnp.log(f_all) * LOG2E
            g_hi, g_lo = _split_bf16(g_all)
        if h % 2 == 0:
            pair = slice(h * dk, (h + 2) * dk)
            g_cat = jnp.concatenate([g_hi[:, pair], g_lo[:, pair]], axis=0)
            ex_pair = jnp.exp2(_dot(w_ref[...], g_cat))
            ex_pair_bf = ex_pair.astype(BF16)
        half = slice((h % 2) * dk, (h % 2 + 1) * dk)
        ex_bf = ex_pair_bf[:, half]
        f_h = f_all[:, cols]
        k_f = 1.0 - f_h
        q_f = _silu(q_ref[0, rws, cols].astype(F32))
        k_bf = k_f.astype(BF16)
        q_bf = q_f.astype(BF16)
        v_bf = i_ref[0, rws, cols]

        scores = jnp.where(lv == 0, jnp.sum(q_f * k_f, axis=-1, keepdims=True), 0.0)
        qd = q_f
        for j in range(1, HG_BAND + 1):
            qd = qd * (f_h if j == 1 else pltpu.roll(f_h, j - 1, axis=0))
            d_j = jnp.sum(qd * pltpu.roll(k_f, j, axis=0), axis=-1, keepdims=True)
            scores = jnp.where(lv == j, d_j, scores)
        for i, lvl in enumerate(HG_MM_LEVELS):
            dec = ex_bf[i * c:(i + 1) * c]
            scores = jnp.where(lv == HG_BAND + lvl, _nt_dot(q_bf * dec, k_bf * dec), scores)

        dec_q = ex_bf[(HG_NMAT - 2) * c:(HG_NMAT - 1) * c]
        dec_k = ex_bf[(HG_NMAT - 1) * c:HG_NMAT * c]
        state = state_ref[h]
        inter = _nt_dot(q_bf * dec_q, state.astype(BF16))
        intra = _dot(scores.astype(BF16), v_bf)
        o = inter + intra

        v_t = v_bf.astype(F32).T.astype(BF16)
        dec_end = ex_pair[(HG_NMAT - 1) * c - 1:(HG_NMAT - 1) * c, half]
        state_ref[h] = state * dec_end + _dot(v_t, k_bf * dec_k)

        ms = jnp.mean(o * o, axis=-1, keepdims=True)
        o = o * lax.rsqrt(ms + RMS_EPS) * nw_ref[...]
        o_ref[0, rws, cols] = (o * _silu(z_ref[0, rws, cols].astype(F32))).astype(o_ref.dtype)


def _hgrn2(proj, lower_bounds, norm_w_l, layer):
    bsz, s, _ = proj.shape
    depth = lower_bounds.shape[0]
    dk = HG_HEAD_DIM
    w = HG_WIDTH
    w_one = _hgrn_sum_matrix()
    w_sum = jnp.asarray(np.concatenate([w_one, w_one], axis=1), BF16)
    lv = jnp.asarray(_hgrn_level_matrix())
    tr = HG_CPS * HG_C
    return pl.pallas_call(
        partial(_hgrn_kernel, layer),
        out_shape=jax.ShapeDtypeStruct((bsz, s, w), BF16),
        grid=(bsz, s // tr),
        in_specs=[
            pl.BlockSpec((1, tr, w), lambda b, i: (b, i, COL_QB)),
            pl.BlockSpec((1, tr, w), lambda b, i: (b, i, COL_FB)),
            pl.BlockSpec((1, tr, w), lambda b, i: (b, i, COL_IB)),
            pl.BlockSpec((1, tr, w), lambda b, i: (b, i, COL_ZB)),
            pl.BlockSpec((depth, w), lambda b, i: (0, 0)),
            pl.BlockSpec((1, dk), lambda b, i: (0, 0)),
            pl.BlockSpec((HG_NMAT * HG_C, 2 * HG_C), lambda b, i: (0, 0)),
            pl.BlockSpec((HG_C, HG_C), lambda b, i: (0, 0)),
        ],
        out_specs=pl.BlockSpec((1, tr, w), lambda b, i: (b, i, 0)),
        scratch_shapes=[pltpu.VMEM((HG_HEADS, dk, dk), F32)],
        compiler_params=pltpu.CompilerParams(
            dimension_semantics=("arbitrary", "arbitrary"),
            vmem_limit_bytes=V7X_VMEM_LIMIT_BYTES),
        name="hgrn2",
    )(proj, proj, proj, proj, lower_bounds, norm_w_l.reshape(1, dk), w_sum, lv)


def _merge_kernel(alpha, emit_h, ya_ref, yb_ref, pre_ref, post_ref, u_ref, zc_ref,
                  pre_h_ref, u_h_ref, ga_ref, gb_ref, gc_ref, x_ref, gate_ref, cw_ref,
                  wb_ref, wo_ref, lng_ref, lnb_ref, *rest):
    if emit_h:
        nmod_ref, xo_ref, h_ref, ext_ref = rest
    else:
        xo_ref, ext_ref = rest
    i = pl.program_id(1)
    tm = pre_ref.shape[1]

    hp = pre_h_ref[0, HALO_ROWS - SUBLANES:HALO_ROWS, :].astype(F32)
    hu = u_h_ref[0, HALO_ROWS - SUBLANES:HALO_ROWS, :].astype(F32)
    ext_ref[0:SUBLANES, :] = jnp.where(i > 0, hp * hu, 0.0)
    ext_ref[SUBLANES:SUBLANES + tm, :] = pre_ref[0].astype(F32) * u_ref[0].astype(F32)
    cw = cw_ref[...]
    conv = (cw[2:3] * ext_ref[SUBLANES:SUBLANES + tm, :]
            + cw[1:2] * ext_ref[SUBLANES - 1:SUBLANES - 1 + tm, :]
            + cw[0:1] * ext_ref[SUBLANES - 2:SUBLANES - 2 + tm, :])
    y_c = (post_ref[0].astype(F32) * conv * _silu(zc_ref[0].astype(F32))).astype(BF16)

    def gate(ref):
        return 1.0 / (1.0 + jnp.exp2(ref[0].astype(F32)))

    merged = (gate(ga_ref) * _dot(ya_ref[0], wb_ref[0])
              + gate(gb_ref) * _dot(yb_ref[0], wb_ref[1])
              + gate(gc_ref) * _dot(y_c, wb_ref[2]))
    y = _dot(merged.astype(BF16), wo_ref[...])

    r = alpha * x_ref[0] + (1.0 + gate_ref[0]) * y
    x_new = _standardize(r) * lng_ref[...] + lnb_ref[...]
    xo_ref[0] = x_new
    if emit_h:
        d = x_new.shape[-1]
        shift = nmod_ref[0, :, 0:d]
        scale = nmod_ref[0, :, d:2 * d]
        h_ref[0] = (_standardize(x_new) * (1.0 + scale) + shift).astype(h_ref.dtype)


def _merge(ya, yb, proj, x, mod_l, conv_w_l, wb_l, wo_l, lng_l, lnb_l, alpha, next_mod):
    bsz, s, d = x.shape
    tm = MERGE_TM
    cw = CONV_WIDTH
    emit_h = next_mod is not None
    halo_blocks = tm // HALO_ROWS

    def halo_map(col):
        return lambda b, i: (b, jnp.maximum(i * halo_blocks - 1, 0), col)

    in_specs = [
        pl.BlockSpec((1, tm, SB_WIDTH), lambda b, i: (b, i, 0)),
        pl.BlockSpec((1, tm, HG_WIDTH), lambda b, i: (b, i, 0)),
        pl.BlockSpec((1, tm, cw), lambda b, i: (b, i, COL_PRE)),
        pl.BlockSpec((1, tm, cw), lambda b, i: (b, i, COL_POST)),
        pl.BlockSpec((1, tm, cw), lambda b, i: (b, i, COL_U)),
        pl.BlockSpec((1, tm, cw), lambda b, i: (b, i, COL_ZC)),
        pl.BlockSpec((1, HALO_ROWS, cw), halo_map(COL_PRE)),
        pl.BlockSpec((1, HALO_ROWS, cw), halo_map(COL_U)),
        pl.BlockSpec((1, tm, d), lambda b, i: (b, i, COL_GATES)),
        pl.BlockSpec((1, tm, d), lambda b, i: (b, i, COL_GATES + 1)),
        pl.BlockSpec((1, tm, d), lambda b, i: (b, i, COL_GATES + 2)),
        pl.BlockSpec((1, tm, d), lambda b, i: (b, i, 0)),
        pl.BlockSpec((1, 1, d), lambda b, i: (b, 0, 2)),
        pl.BlockSpec((CONV_K, cw), lambda b, i: (0, 0)),
        pl.BlockSpec((3, SB_WIDTH, d), lambda b, i: (0, 0, 0)),
        pl.BlockSpec((d, d), lambda b, i: (0, 0)),
        pl.BlockSpec((1, d), lambda b, i: (0, 0)),
        pl.BlockSpec((1, d), lambda b, i: (0, 0)),
    ]
    args = [ya, yb] + [proj] * 9 + [x, mod_l, conv_w_l, wb_l, wo_l,
            lng_l.reshape(1, d), lnb_l.reshape(1, d)]
    out_shape = [jax.ShapeDtypeStruct((bsz, s, d), F32)]
    out_specs = [pl.BlockSpec((1, tm, d), lambda b, i: (b, i, 0))]
    if emit_h:
        in_specs.append(pl.BlockSpec((1, 1, 3 * d), lambda b, i: (b, 0, 0)))
        args.append(next_mod)
        out_shape.append(jax.ShapeDtypeStruct((bsz, s, d), BF16))
        out_specs.append(pl.BlockSpec((1, tm, d), lambda b, i: (b, i, 0)))
    outs = pl.pallas_call(
        partial(_merge_kernel, alpha, emit_h),
        out_shape=out_shape,
        grid=(bsz, s // tm),
        in_specs=in_specs,
        out_specs=out_specs,
        scratch_shapes=[pltpu.VMEM((tm + SUBLANES, cw), F32)],
        compiler_params=pltpu.CompilerParams(
            dimension_semantics=("arbitrary", "arbitrary"),
            vmem_limit_bytes=V7X_VMEM_LIMIT_BYTES),
        name="merge_residual",
    )(*args)
    return (outs[0], outs[1]) if emit_h else (outs[0], None)


def _in_proj_col_scale():
    cs = np.ones((1, IN_COLS), np.float32)
    cs[:, COL_QA * COL_GROUP:(COL_QA + 1) * COL_GROUP] = LOG2E * SB_HEAD_DIM ** -0.5
    cs[:, COL_GATES * D_MODEL:] = -LOG2E
    return jnp.asarray(cs)


def kernel(x, c, w_mod, b_mod, w_in, conv_w, hgrn_norm_w, lower_bounds, w_branch, w_out, ln_g, ln_b):
    bsz, s, d = x.shape
    depth = w_mod.shape[0]
    alpha = (2.0 * depth) ** 0.25

    mod = _modulation(c, w_mod, b_mod).reshape(depth, bsz, 1, 3 * d)
    col_scale = _in_proj_col_scale()

    h = _lnmod(x, mod[0])
    for l in range(depth):
        proj = _in_proj(h.reshape(bsz * s, d), w_in, l, col_scale).reshape(bsz, s, IN_COLS)
        ya = _stick_breaking(proj)
        yb = _hgrn2(proj, lower_bounds, hgrn_norm_w[l], l)
        next_mod = mod[l + 1] if l + 1 < depth else None
        x, h = _merge(ya, yb, proj, x, mod[l], conv_w[l], w_branch[l].astype(BF16),
                      w_out[l].astype(BF16), ln_g[l], ln_b[l], alpha, next_mod)
    return x
```

```python
from functools import partial

import numpy as np
import jax
import jax.numpy as jnp
from jax import lax
from jax.experimental import pallas as pl
from jax.experimental.pallas import tpu as pltpu

D_MODEL = 1024
SB_HEADS = 8
SB_HEAD_DIM = 64
SB_WIDTH = SB_HEADS * SB_HEAD_DIM
HG_HEADS = 4
HG_HEAD_DIM = 128
HG_WIDTH = HG_HEADS * HG_HEAD_DIM
CONV_WIDTH = 512
CONV_K = 3
LN_EPS = 1e-5
RMS_EPS = 1e-6

F32 = jnp.float32
BF16 = jnp.bfloat16

V7X_VMEM_LIMIT_BYTES = 56 * 1024 * 1024
SUBLANES = 8
LANES = 128
HALO_ROWS = 16

COL_GROUP = SB_WIDTH
N_COL_GROUPS = 12
IN_COLS = N_COL_GROUPS * COL_GROUP + 3 * D_MODEL
(COL_QA, COL_KA, COL_VA, COL_ZA, COL_QB, COL_FB, COL_IB, COL_ZB,
 COL_PRE, COL_POST, COL_U, COL_ZC) = range(N_COL_GROUPS)
COL_GATES = N_COL_GROUPS * COL_GROUP // D_MODEL
IN_PROJ_TM = 2048
IN_PROJ_TN = 1536
MERGE_TM = 512
MERGE_TN = 256
LNMOD_TM = 2048

SB_GROUP = 4
SB_GROUP_W = SB_GROUP * SB_HEAD_DIM
SB_N_GROUPS = SB_HEADS // SB_GROUP
SB_TQ = 512
SB_TK = 256
SB_N_DIAG = SB_TQ // SB_TK
SB_PART_ROWS = SB_GROUP * SB_TK
SB_BPS = 2
SB_UNDERFLOW_BITS = 160.0
SB_BOUND_SLACK = 1.05
SB_CHAINS = [(bb, g) for bb in range(SB_BPS) for g in range(SB_N_GROUPS)]
LOG2E = 1.4426950408889634
LOG2E_BF16_HI = 1.4453125
LOG2E_BF16_LO = -0.00262451171875

HG_C = 128
HG_CPS = 8
HG_LEVELS = 7
HG_BAND = 3
HG_MM_LEVELS = range(3, HG_LEVELS + 1)
HG_NMAT = len(HG_MM_LEVELS) + 2


def _nt_dot(a, b):
    return lax.dot_general(a, b, (((1,), (1,)), ((), ())), preferred_element_type=F32)


def _dot(a, b):
    return jnp.dot(a, b, preferred_element_type=F32)


def _split_bf16(v):
    hi = v.astype(BF16)
    lo = (v - hi.astype(F32)).astype(BF16)
    return hi, lo


def _sigmoid(v):
    return 1.0 / (1.0 + jnp.exp(-v))


def _silu(v):
    return v * _sigmoid(v)


def _standardize(xf):
    mu = jnp.mean(xf, axis=-1, keepdims=True)
    xc = xf - mu
    var = jnp.mean(xc * xc, axis=-1, keepdims=True)
    return xc * lax.rsqrt(var + LN_EPS)


def _mod_kernel(c_ref, w_ref, b_ref, o_ref):
    c_hi, c_lo = _split_bf16(c_ref[...])
    w_hi, w_lo = _split_bf16(w_ref[0])
    acc = _dot(c_hi, w_hi) + (_dot(c_hi, w_lo) + _dot(c_lo, w_hi))
    o_ref[0] = acc + b_ref[0]


def _modulation(c, w_mod, b_mod):
    depth, d, n = w_mod.shape
    bsz = c.shape[0]
    tn = 1024
    return pl.pallas_call(
        _mod_kernel,
        out_shape=jax.ShapeDtypeStruct((depth, bsz, n), F32),
        grid=(depth, n // tn),
        in_specs=[
            pl.BlockSpec((bsz, d), lambda l, j: (0, 0)),
            pl.BlockSpec((1, d, tn), lambda l, j: (l, 0, j)),
            pl.BlockSpec((1, 1, tn), lambda l, j: (l, 0, j)),
        ],
        out_specs=pl.BlockSpec((1, bsz, tn), lambda l, j: (l, 0, j)),
        compiler_params=pltpu.CompilerParams(
            dimension_semantics=("arbitrary", "arbitrary"),
            vmem_limit_bytes=V7X_VMEM_LIMIT_BYTES),
        name="adaln_mod",
    )(c, w_mod, b_mod.reshape(depth, 1, n))


def _lnmod_kernel(x_ref, mod_ref, h_ref):
    d = x_ref.shape[-1]
    shift = mod_ref[0, :, 0:d]
    scale = mod_ref[0, :, d:2 * d]
    h_ref[0] = (_standardize(x_ref[0]) * (1.0 + scale) + shift).astype(h_ref.dtype)


def _lnmod(x, mod_l):
    bsz, s, d = x.shape
    tm = LNMOD_TM
    return pl.pallas_call(
        _lnmod_kernel,
        out_shape=jax.ShapeDtypeStruct((bsz, s, d), BF16),
        grid=(bsz, s // tm),
        in_specs=[
            pl.BlockSpec((1, tm, d), lambda b, i: (b, i, 0)),
            pl.BlockSpec((1, 1, 3 * d), lambda b, i: (b, 0, 0)),
        ],
        out_specs=pl.BlockSpec((1, tm, d), lambda b, i: (b, i, 0)),
        compiler_params=pltpu.CompilerParams(
            dimension_semantics=("arbitrary", "arbitrary"),
            vmem_limit_bytes=V7X_VMEM_LIMIT_BYTES),
        name="ln_modulate",
    )(x, mod_l)


def _in_proj_kernel(h_ref, w_ref, cs_ref, o_ref, wbf_ref):
    @pl.when(pl.program_id(1) == 0)
    def _():
        wbf_ref[...] = (w_ref[0] * cs_ref[...]).astype(BF16)

    o_ref[...] = _dot(h_ref[...], wbf_ref[...]).astype(o_ref.dtype)


def _in_proj(h2, w_in, layer, col_scale):
    m, k = h2.shape
    n = w_in.shape[2]
    tm, tn = IN_PROJ_TM, IN_PROJ_TN
    return pl.pallas_call(
        _in_proj_kernel,
        out_shape=jax.ShapeDtypeStruct((m, n), BF16),
        grid=(n // tn, m // tm),
        in_specs=[
            pl.BlockSpec((tm, k), lambda j, i: (i, 0)),
            pl.BlockSpec((1, k, tn), lambda j, i: (layer, 0, j)),
            pl.BlockSpec((1, tn), lambda j, i: (0, j)),
        ],
        out_specs=pl.BlockSpec((tm, tn), lambda j, i: (i, j)),
        scratch_shapes=[pltpu.VMEM((k, tn), BF16)],
        compiler_params=pltpu.CompilerParams(
            dimension_semantics=("arbitrary", "arbitrary"),
            vmem_limit_bytes=V7X_VMEM_LIMIT_BYTES),
        name="in_proj",
    )(h2, w_in, col_scale)


def _sb_band_mask():
    t_in = np.arange(SB_PART_ROWS)[:, None] % SB_TK
    return (np.arange(SB_TK)[None, :] < t_in).astype(np.float32)


def _sb_kernel(q_ref, k_ref, v_ref, z_ref, dm_ref, o_ref, qs_ref, acc_ref, car_ref, kmax2_ref,
               cmin_ref):
    qi = pl.program_id(1)
    tk, gw = SB_TK, SB_GROUP_W
    rows = SB_N_DIAG * SB_PART_ROWS

    lane_head = lax.broadcasted_iota(jnp.int32, (tk, gw), 1) // SB_HEAD_DIM
    head_row = lax.broadcasted_iota(jnp.int32, (1, gw), 1) // SB_HEAD_DIM
    head_keep = [jnp.where(head_row == h, 1.0, 0.0).astype(BF16) for h in range(SB_GROUP)]
    for c, (bb, g) in enumerate(SB_CHAINS):
        for p in range(SB_N_DIAG):
            q_bf = q_ref[bb, p * tk:(p + 1) * tk, g * gw:(g + 1) * gw]
            qs_ref[c, p * SB_PART_ROWS:(p + 1) * SB_PART_ROWS, :] = jnp.concatenate(
                [q_bf * head_keep[h] for h in range(SB_GROUP)], axis=0)

    jj = lax.broadcasted_iota(jnp.int32, (tk, tk), 0)
    ss = lax.broadcasted_iota(jnp.int32, (tk, tk), 1)
    u_mat = jnp.where(jj >= ss, 1.0, 0.0).astype(BF16)

    def softplus2_bf16(z):
        zb = z.astype(BF16)
        ln_w = jnp.log(1.0 + jnp.exp2(-jnp.abs(zb)))
        return jnp.maximum(zb, 0.0) + (ln_w * LOG2E_BF16_HI + ln_w * LOG2E_BF16_LO)

    def block(kb, band, r_hi=rows):
        start = pl.multiple_of(kb * tk, tk)
        r0 = 0 if band is None else band * SB_PART_ROWS
        rws = slice(r0, r_hi)
        for c, (bb, g) in enumerate(SB_CHAINS):
            gcols = slice(g * gw, (g + 1) * gw)
            z = _nt_dot(qs_ref[c, rws, :], k_ref[bb, pl.ds(start, tk), gcols])
            sp = softplus2_bf16(z)
            if band is not None:
                r1 = r0 + SB_PART_ROWS
                tri = dm_ref[...]
                sp_top = sp[0:SB_PART_ROWS] * tri
                sp = sp_top if r1 == rows else jnp.concatenate([sp_top, sp[SB_PART_ROWS:]], axis=0)
            rb = _dot(sp, u_mat)
            if band is None:
                car = car_ref[c, rws, :]
            else:
                car = jnp.zeros((SB_PART_ROWS, 1), F32)
                if r1 < rows:
                    car = jnp.concatenate([car, car_ref[c, r1:rows, :]], axis=0)
            arg = z - (rb + car)
            if band is not None:
                a = jnp.exp2(jnp.minimum(arg[0:SB_PART_ROWS], 0.0).astype(BF16)) * tri
                if r1 < rows:
                    a = jnp.concatenate([a, jnp.exp2(arg[SB_PART_ROWS:].astype(BF16))], axis=0)
            else:
                a = jnp.exp2(arg.astype(BF16))
            av = _dot(a, v_ref[bb, pl.ds(start, tk), gcols])
            if band is None:
                acc_ref[c, rws, :] += av
            else:
                acc_ref[c, r0:r1, :] = av[0:SB_PART_ROWS]
                if r1 < rows:
                    acc_ref[c, r1:rows, :] += av[SB_PART_ROWS:]
            new_car = car + rb[:, 0:1]
            car_ref[c, rws, :] = new_car
            for p in range(r0 // SB_PART_ROWS, r_hi // SB_PART_ROWS):
                lo = p * SB_PART_ROWS - r0
                cmin_ref[c * SB_N_DIAG + p] = jnp.min(new_car[lo:lo + SB_PART_ROWS])

    first_kb = qi * SB_N_DIAG
    for d in reversed(range(SB_N_DIAG)):
        block(first_kb + d, d)

    @pl.when(qi == 0)
    def _():
        for c, (bb, g) in enumerate(SB_CHAINS):
            kf = k_ref[bb, :, g * gw:(g + 1) * gw].astype(F32)
            kmax2_ref[c] = jnp.max(jnp.sum(kf * kf, axis=-1, keepdims=True))

    seg_l = lax.broadcasted_iota(jnp.int32, (gw, LANES), 0) // SB_HEAD_DIM
    seg_j = lax.broadcasted_iota(jnp.int32, (gw, LANES), 1)
    seg = jnp.where(seg_l == seg_j, 1.0, 0.0).astype(BF16)
    finished_at = []
    for c, (bb, g) in enumerate(SB_CHAINS):
        qf = q_ref[bb, :, g * gw:(g + 1) * gw].astype(F32)
        qmax2 = jnp.max(_dot((qf * qf).astype(BF16), seg))
        zmax = jnp.sqrt(jnp.full((SUBLANES, LANES), qmax2 * kmax2_ref[c] * SB_BOUND_SLACK, F32))
        finished_at.append(jnp.max(zmax) + SB_UNDERFLOW_BITS)

    def flags():
        go_head = jnp.bool_(False)
        go_tail = jnp.bool_(False)
        for c in range(len(SB_CHAINS)):
            go_head = jnp.logical_or(go_head, cmin_ref[c * SB_N_DIAG] <= finished_at[c])
            for p in range(1, SB_N_DIAG):
                go_tail = jnp.logical_or(go_tail, cmin_ref[c * SB_N_DIAG + p] <= finished_at[c])
        return go_head, go_tail

    def cond(carry):
        i, go_head, go_tail = carry
        return jnp.logical_and(i < first_kb, jnp.logical_or(go_head, go_tail))

    def body(carry):
        i, _, go_tail = carry
        kb = first_kb - 1 - i

        @pl.when(go_tail)
        def _():
            block(kb, None)

        @pl.when(jnp.logical_not(go_tail))
        def _():
            block(kb, None, SB_PART_ROWS)

        return (i + 1,) + flags()

    lax.while_loop(cond, body, (jnp.int32(0),) + flags())

    for c, (bb, g) in enumerate(SB_CHAINS):
        for p in range(SB_N_DIAG):
            r = p * SB_PART_ROWS
            o = acc_ref[c, r:r + tk, :]
            for h in range(1, SB_GROUP):
                o = jnp.where(lane_head == h, acc_ref[c, r + h * tk:r + (h + 1) * tk, :], o)
            zg = z_ref[bb, p * tk:(p + 1) * tk, g * gw:(g + 1) * gw].astype(F32)
            o_ref[bb, p * tk:(p + 1) * tk, g * gw:(g + 1) * gw] = (
                o * _silu(zg)).astype(o_ref.dtype)


def _stick_breaking(proj):
    bsz, s, _ = proj.shape
    w = SB_WIDTH
    n_chains = len(SB_CHAINS)
    rows = SB_N_DIAG * SB_PART_ROWS
    return pl.pallas_call(
        _sb_kernel,
        out_shape=jax.ShapeDtypeStruct((bsz, s, w), BF16),
        grid=(bsz // SB_BPS, s // SB_TQ),
        in_specs=[
            pl.BlockSpec((SB_BPS, SB_TQ, w), lambda b, i: (b, i, COL_QA)),
            pl.BlockSpec((SB_BPS, s, w), lambda b, i: (b, 0, COL_KA)),
            pl.BlockSpec((SB_BPS, s, w), lambda b, i: (b, 0, COL_VA)),
            pl.BlockSpec((SB_BPS, SB_TQ, w), lambda b, i: (b, i, COL_ZA)),
            pl.BlockSpec((SB_PART_ROWS, SB_TK), lambda b, i: (0, 0)),
        ],
        out_specs=pl.BlockSpec((SB_BPS, SB_TQ, w), lambda b, i: (b, i, 0)),
        scratch_shapes=[
            pltpu.VMEM((n_chains, rows, SB_GROUP_W), BF16),
            pltpu.VMEM((n_chains, rows, SB_GROUP_W), F32),
            pltpu.VMEM((n_chains, rows, 1), F32),
            pltpu.SMEM((n_chains,), F32),
            pltpu.SMEM((n_chains * SB_N_DIAG,), F32),
        ],
        compiler_params=pltpu.CompilerParams(
            dimension_semantics=("arbitrary", "arbitrary"),
            vmem_limit_bytes=V7X_VMEM_LIMIT_BYTES),
        name="stick_breaking",
    )(proj, proj, proj, proj, jnp.asarray(_sb_band_mask(), BF16))


def _hgrn_sum_matrix():
    c = HG_C
    w = np.zeros((HG_NMAT, c, c), np.float32)
    for i, lvl in enumerate(HG_MM_LEVELS):
        n = 1 << lvl
        for r in range(c):
            m = (r // n) * n + n // 2
            if r >= m:
                w[i, r, m:r + 1] = 1.0
            else:
                w[i, r, r + 1:m] = 1.0
    for r in range(c):
        w[HG_NMAT - 2, r, :r + 1] = 1.0
        w[HG_NMAT - 1, r, r + 1:] = 1.0
    return w.reshape(HG_NMAT * c, c)


def _hgrn_level_matrix():
    c = HG_C
    lv = np.full((c, c), -1, np.int32)
    for t in range(c):
        for s in range(t + 1):
            lv[t, s] = t - s if t - s <= HG_BAND else HG_BAND + (t ^ s).bit_length()
    return lv


def _hgrn_kernel(layer, q_ref, f_ref, i_ref, z_ref, lb_ref, nw_ref, w_ref, lv_ref, o_ref,
                 state_ref):
    c = HG_C
    dk = HG_HEAD_DIM
    ci = pl.program_id(1)

    @pl.when(ci == 0)
    def _():
        state_ref[...] = jnp.zeros_like(state_ref)

    lbw = lb_ref[...]
    e = jnp.exp(lbw - jnp.max(lbw, axis=0, keepdims=True))
    p = e / jnp.sum(e, axis=0, keepdims=True)
    lb_all = jnp.zeros((1, HG_WIDTH), F32)
    for l in range(1, layer + 1):
        lb_all = lb_all + p[l:l + 1]

    lv = lv_ref[...]
    for cc, h in [(cc, h) for cc in range(HG_CPS) for h in range(HG_HEADS)]:
        rws = slice(cc * c, (cc + 1) * c)
        cols = slice(h * dk, (h + 1) * dk)
        if h == 0:
            f_all = lb_all + (1.0 - lb_all) * _sigmoid(f_ref[0, rws, :].astype(F32))
            g_all = jnp.log(f_all) * LOG2E
            g_hi, g_lo = _split_bf16(g_all)
        if h % 2 == 0:
            pair = slice(h * dk, (h + 2) * dk)
            g_cat = jnp.concatenate([g_hi[:, pair], g_lo[:, pair]], axis=0)
            ex_pair = jnp.exp2(_dot(w_ref[...], g_cat))
            ex_pair_bf = ex_pair.astype(BF16)
        half = slice((h % 2) * dk, (h % 2 + 1) * dk)
        ex_bf = ex_pair_bf[:, half]
        f_h = f_all[:, cols]
        k_f = 1.0 - f_h
        q_f = _silu(q_ref[0, rws, cols].astype(F32))
        k_bf = k_f.astype(BF16)
        q_bf = q_f.astype(BF16)
        v_bf = i_ref[0, rws, cols]

        scores = jnp.where(lv == 0, jnp.sum(q_f * k_f, axis=-1, keepdims=True), 0.0)
        qd = q_f
        for j in range(1, HG_BAND + 1):
            qd = qd * (f_h if j == 1 else pltpu.roll(f_h, j - 1, axis=0))
            d_j = jnp.sum(qd * pltpu.roll(k_f, j, axis=0), axis=-1, keepdims=True)
            scores = jnp.where(lv == j, d_j, scores)
        for i, lvl in enumerate(HG_MM_LEVELS):
            dec = ex_bf[i * c:(i + 1) * c]
            scores = jnp.where(lv == HG_BAND + lvl, _nt_dot(q_bf * dec, k_bf * dec), scores)

        dec_q = ex_bf[(HG_NMAT - 2) * c:(HG_NMAT - 1) * c]
        dec_k = ex_bf[(HG_NMAT - 1) * c:HG_NMAT * c]
        state = state_ref[h]
        inter = _nt_dot(q_bf * dec_q, state.astype(BF16))
        intra = _dot(scores.astype(BF16), v_bf)
        o = inter + intra

        v_t = v_bf.astype(F32).T.astype(BF16)
        dec_end = ex_pair[(HG_NMAT - 1) * c - 1:(HG_NMAT - 1) * c, half]
        state_ref[h] = state * dec_end + _dot(v_t, k_bf * dec_k)

        ms = jnp.mean(o * o, axis=-1, keepdims=True)
        o = o * lax.rsqrt(ms + RMS_EPS) * nw_ref[...]
        o_ref[0, rws, cols] = (o * _silu(z_ref[0, rws, cols].astype(F32))).astype(o_ref.dtype)


def _hgrn2(proj, lower_bounds, norm_w_l, layer):
    bsz, s, _ = proj.shape
    depth = lower_bounds.shape[0]
    dk = HG_HEAD_DIM
    w = HG_WIDTH
    w_one = _hgrn_sum_matrix()
    w_sum = jnp.asarray(np.concatenate([w_one, w_one], axis=1), BF16)
    lv = jnp.asarray(_hgrn_level_matrix())
    tr = HG_CPS * HG_C
    return pl.pallas_call(
        partial(_hgrn_kernel, layer),
        out_shape=jax.ShapeDtypeStruct((bsz, s, w), BF16),
        grid=(bsz, s // tr),
        in_specs=[
            pl.BlockSpec((1, tr, w), lambda b, i: (b, i, COL_QB)),
            pl.BlockSpec((1, tr, w), lambda b, i: (b, i, COL_FB)),
            pl.BlockSpec((1, tr, w), lambda b, i: (b, i, COL_IB)),
            pl.BlockSpec((1, tr, w), lambda b, i: (b, i, COL_ZB)),
            pl.BlockSpec((depth, w), lambda b, i: (0, 0)),
            pl.BlockSpec((1, dk), lambda b, i: (0, 0)),
            pl.BlockSpec((HG_NMAT * HG_C, 2 * HG_C), lambda b, i: (0, 0)),
            pl.BlockSpec((HG_C, HG_C), lambda b, i: (0, 0)),
        ],
        out_specs=pl.BlockSpec((1, tr, w), lambda b, i: (b, i, 0)),
        scratch_shapes=[pltpu.VMEM((HG_HEADS, dk, dk), F32)],
        compiler_params=pltpu.CompilerParams(
            dimension_semantics=("arbitrary", "arbitrary"),
            vmem_limit_bytes=V7X_VMEM_LIMIT_BYTES),
        name="hgrn2",
    )(proj, proj, proj, proj, lower_bounds, norm_w_l.reshape(1, dk), w_sum, lv)


def _merge_kernel(alpha, emit_h, ya_ref, yb_ref, pre_ref, post_ref, u_ref, zc_ref,
                  pre_h_ref, u_h_ref, ga_ref, gb_ref, gc_ref, x_ref, gate_ref, cw_ref,
                  wb_ref, wo_ref, lng_ref, lnb_ref, *rest):
    if emit_h:
        nmod_ref, xo_ref, h_ref, ext_ref = rest
    else:
        xo_ref, ext_ref = rest
    i = pl.program_id(1)
    tm = pre_ref.shape[1]

    hp = pre_h_ref[0, HALO_ROWS - SUBLANES:HALO_ROWS, :].astype(F32)
    hu = u_h_ref[0, HALO_ROWS - SUBLANES:HALO_ROWS, :].astype(F32)
    ext_ref[0:SUBLANES, :] = jnp.where(i > 0, hp * hu, 0.0)
    ext_ref[SUBLANES:SUBLANES + tm, :] = pre_ref[0].astype(F32) * u_ref[0].astype(F32)
    cw = cw_ref[...]
    conv = (cw[2:3] * ext_ref[SUBLANES:SUBLANES + tm, :]
            + cw[1:2] * ext_ref[SUBLANES - 1:SUBLANES - 1 + tm, :]
            + cw[0:1] * ext_ref[SUBLANES - 2:SUBLANES - 2 + tm, :])
    y_c = (post_ref[0].astype(F32) * conv * _silu(zc_ref[0].astype(F32))).astype(BF16)

    def gate(ref, cols):
        return 1.0 / (1.0 + jnp.exp2(ref[0, :, cols].astype(F32)))

    ya, yb = ya_ref[0], yb_ref[0]
    merged_cols = wo_ref.shape[0]
    tiles = []
    for n in range(merged_cols // MERGE_TN):
        cols = slice(n * MERGE_TN, (n + 1) * MERGE_TN)
        tiles.append((gate(ga_ref, cols) * _dot(ya, wb_ref[0, :, cols])
                      + gate(gb_ref, cols) * _dot(yb, wb_ref[1, :, cols])
                      + gate(gc_ref, cols) * _dot(y_c, wb_ref[2, :, cols])).astype(BF16))
    y = _dot(jnp.concatenate(tiles, axis=1), wo_ref[...])

    r = alpha * x_ref[0] + (1.0 + gate_ref[0]) * y
    x_new = _standardize(r) * lng_ref[...] + lnb_ref[...]
    xo_ref[0] = x_new
    if emit_h:
        d = x_new.shape[-1]
        shift = nmod_ref[0, :, 0:d]
        scale = nmod_ref[0, :, d:2 * d]
        h_ref[0] = (_standardize(x_new) * (1.0 + scale) + shift).astype(h_ref.dtype)


def _merge(ya, yb, proj, x, mod_l, conv_w_l, wb_l, wo_l, lng_l, lnb_l, alpha, next_mod):
    bsz, s, d = x.shape
    tm = MERGE_TM
    cw = CONV_WIDTH
    emit_h = next_mod is not None
    halo_blocks = tm // HALO_ROWS

    def halo_map(col):
        return lambda b, i: (b, jnp.maximum(i * halo_blocks - 1, 0), col)

    in_specs = [
        pl.BlockSpec((1, tm, SB_WIDTH), lambda b, i: (b, i, 0)),
        pl.BlockSpec((1, tm, HG_WIDTH), lambda b, i: (b, i, 0)),
        pl.BlockSpec((1, tm, cw), lambda b, i: (b, i, COL_PRE)),
        pl.BlockSpec((1, tm, cw), lambda b, i: (b, i, COL_POST)),
        pl.BlockSpec((1, tm, cw), lambda b, i: (b, i, COL_U)),
        pl.BlockSpec((1, tm, cw), lambda b, i: (b, i, COL_ZC)),
        pl.BlockSpec((1, HALO_ROWS, cw), halo_map(COL_PRE)),
        pl.BlockSpec((1, HALO_ROWS, cw), halo_map(COL_U)),
        pl.BlockSpec((1, tm, d), lambda b, i: (b, i, COL_GATES)),
        pl.BlockSpec((1, tm, d), lambda b, i: (b, i, COL_GATES + 1)),
        pl.BlockSpec((1, tm, d), lambda b, i: (b, i, COL_GATES + 2)),
        pl.BlockSpec((1, tm, d), lambda b, i: (b, i, 0)),
        pl.BlockSpec((1, 1, d), lambda b, i: (b, 0, 2)),
        pl.BlockSpec((CONV_K, cw), lambda b, i: (0, 0)),
        pl.BlockSpec((3, SB_WIDTH, d), lambda b, i: (0, 0, 0)),
        pl.BlockSpec((d, d), lambda b, i: (0, 0)),
        pl.BlockSpec((1, d), lambda b, i: (0, 0)),
        pl.BlockSpec((1, d), lambda b, i: (0, 0)),
    ]
    args = [ya, yb] + [proj] * 9 + [x, mod_l, conv_w_l, wb_l, wo_l,
            lng_l.reshape(1, d), lnb_l.reshape(1, d)]
    out_shape = [jax.ShapeDtypeStruct((bsz, s, d), F32)]
    out_specs = [pl.BlockSpec((1, tm, d), lambda b, i: (b, i, 0))]
    if emit_h:
        in_specs.append(pl.BlockSpec((1, 1, 3 * d), lambda b, i: (b, 0, 0)))
        args.append(next_mod)
        out_shape.append(jax.ShapeDtypeStruct((bsz, s, d), BF16))
        out_specs.append(pl.BlockSpec((1, tm, d), lambda b, i: (b, i, 0)))
    outs = pl.pallas_call(
        partial(_merge_kernel, alpha, emit_h),
        out_shape=out_shape,
        grid=(bsz, s // tm),
        in_specs=in_specs,
        out_specs=out_specs,
        scratch_shapes=[pltpu.VMEM((tm + SUBLANES, cw), F32)],
        compiler_params=pltpu.CompilerParams(
            dimension_semantics=("arbitrary", "arbitrary"),
            vmem_limit_bytes=V7X_VMEM_LIMIT_BYTES),
        name="merge_residual",
    )(*args)
    return (outs[0], outs[1]) if emit_h else (outs[0], None)


def _in_proj_col_scale():
    cs = np.ones((1, IN_COLS), np.float32)
    cs[:, COL_QA * COL_GROUP:(COL_QA + 1) * COL_GROUP] = LOG2E * SB_HEAD_DIM ** -0.5
    cs[:, COL_GATES * D_MODEL:] = -LOG2E
    return jnp.asarray(cs)


def kernel(x, c, w_mod, b_mod, w_in, conv_w, hgrn_norm_w, lower_bounds, w_branch, w_out, ln_g, ln_b):
    bsz, s, d = x.shape
    depth = w_mod.shape[0]
    alpha = (2.0 * depth) ** 0.25

    mod = _modulation(c, w_mod, b_mod).reshape(depth, bsz, 1, 3 * d)
    col_scale = _in_proj_col_scale()

    h = _lnmod(x, mod[0])
    for l in range(depth):
        proj = _in_proj(h.reshape(bsz * s, d), w_in, l, col_scale).reshape(bsz, s, IN_COLS)
        ya = _stick_breaking(proj)
        yb = _hgrn2(proj, lower_bounds, hgrn_norm_w[l], l)
        next_mod = mod[l + 1] if l + 1 < depth else None
        x, h = _merge(ya, yb, proj, x, mod[l], conv_w[l], w_branch[l].astype(BF16),
                      w_out[l].astype(BF16), ln_g[l], ln_b[l], alpha, next_mod)
    return x
```
